```python
import jax
import jax.numpy as jnp
from jax import lax
import numpy as np

D_MODEL = 2048
BATCH = 2
SEQ = 4096
DEPTH = 2

D_PLE = 256
D_FF = 5632
EPS = 1e-6

SG_HEADS = 8
SG_HEAD_DIM = 128
SG_WIDTH = SG_HEADS * SG_HEAD_DIM
SG_CHUNK = 128

GLA_HEADS = 4
GLA_DK = 128
GLA_DV = 256
GLA_KW = GLA_HEADS * GLA_DK
GLA_VW = GLA_HEADS * GLA_DV
GLA_GATE_RANK = 16
GLA_GATE_TAU = 16.0
GLA_CHUNK = 64

MIX_WIDTH = SG_WIDTH + GLA_VW
IN_WIDTH = 2 * SG_WIDTH + 2 * GLA_KW + 2 * GLA_VW + GLA_GATE_RANK
MIX_SPLITS = (SG_WIDTH, 2 * SG_WIDTH, 2 * SG_WIDTH + GLA_KW, 2 * SG_WIDTH + 2 * GLA_KW,
              2 * SG_WIDTH + 2 * GLA_KW + GLA_VW, 2 * SG_WIDTH + 2 * GLA_KW + 2 * GLA_VW)

kernel_name = 'hybrid_sgmlp_gla_macaron_ple'


def rms_norm(x, g):
    xf = x.astype(jnp.float32)
    y = xf * lax.rsqrt(jnp.mean(xf * xf, axis=-1, keepdims=True) + EPS)
    return (y * g.astype(jnp.float32)).astype(x.dtype)


def layer_norm(x, g):
    xf = x.astype(jnp.float32)
    xc = xf - jnp.mean(xf, axis=-1, keepdims=True)
    y = xc * lax.rsqrt(jnp.mean(xc * xc, axis=-1, keepdims=True) + EPS)
    return (y * g.astype(jnp.float32)).astype(x.dtype)


def swiglu_ffn(x, w_in, w_out):
    gate, up = jnp.split(x @ w_in, 2, axis=-1)
    return (jax.nn.silu(gate) * up) @ w_out


def chunked_spatial_gating(u, v, v_gain, w_s, b_s):
    bsz, t_len, _ = u.shape
    n_chunks = t_len // SG_CHUNK
    u = u.reshape(bsz, n_chunks, SG_CHUNK, SG_HEADS, SG_HEAD_DIM)
    v = v.reshape(bsz, n_chunks, SG_CHUNK, SG_HEADS, SG_HEAD_DIM)
    v = layer_norm(v, v_gain.reshape(SG_HEADS, SG_HEAD_DIM))
    causal = jnp.tril(jnp.ones((SG_CHUNK, SG_CHUNK), dtype=bool))
    w = jnp.where(causal, w_s, jnp.zeros_like(w_s))
    mixed = jnp.einsum('hts,bnshd->bnthd', w, v) + b_s.T[None, None, :, :, None]
    return (u * mixed).reshape(bsz, t_len, SG_WIDTH)


def gla_chunked(q, k, v, log_a):
    out_dtype = v.dtype
    bsz, n_heads, t_len, dk = q.shape
    dv = v.shape[-1]
    n_chunks = t_len // GLA_CHUNK

    def to_chunks(t):
        t = t.astype(jnp.float32).reshape(bsz, n_heads, n_chunks, GLA_CHUNK, t.shape[-1])
        return jnp.moveaxis(t, 2, 0)

    qc, kc, vc, gc = to_chunks(q), to_chunks(k), to_chunks(v), to_chunks(log_a)
    causal = jnp.tril(jnp.ones((GLA_CHUNK, GLA_CHUNK), dtype=bool))[:, :, None]

    def step(state, inp):
        qi, ki, vi, gi = inp
        b = jnp.cumsum(gi, axis=2)
        o_inter = jnp.einsum('bhck,bhkv->bhcv', qi * jnp.exp(b), state)
        rel = jnp.where(causal, b[:, :, :, None, :] - b[:, :, None, :, :], -jnp.inf)
        scores = jnp.einsum('bhik,bhjk,bhijk->bhij', qi, ki, jnp.exp(rel))
        o = o_inter + jnp.einsum('bhij,bhjv->bhiv', scores, vi)
        b_last = b[:, :, -1:, :]
        k_dec = ki * jnp.exp(b_last - b)
        new_state = state * jnp.exp(b_last)[:, :, 0, :, None] + jnp.einsum('bhck,bhcv->bhkv', k_dec, vi)
        return new_state, o

    state0 = jnp.zeros((bsz, n_heads, dk, dv), jnp.float32)
    _, o = lax.scan(step, state0, (qc, kc, vc, gc))
    return jnp.moveaxis(o, 0, 2).reshape(bsz, n_heads, t_len, dv).astype(out_dtype)


def gla_mixer(q, k, v, r, z, w_gate, b_gate, o_gain):
    bsz, t_len, _ = q.shape
    log_a = jax.nn.log_sigmoid((z @ w_gate + b_gate).astype(jnp.float32)) / GLA_GATE_TAU

    def heads(t, d):
        return t.reshape(bsz, t_len, GLA_HEADS, d).transpose(0, 2, 1, 3)

    o = gla_chunked(heads(q * GLA_DK ** -0.5, GLA_DK), heads(k, GLA_DK),
                    heads(v, GLA_DV), heads(log_a, GLA_DK))
    o = o.transpose(0, 2, 1, 3)
    o = rms_norm(o, o_gain.reshape(GLA_HEADS, GLA_DV))
    return o.reshape(bsz, t_len, GLA_VW) * jax.nn.silu(r)


def setup_inputs(seed: int = 0) -> dict:
    key = jax.random.key(seed)
    ks = jax.random.split(key, 24)
    f32 = jnp.float32
    L, D = DEPTH, D_MODEL

    def normal(k, shape, scale):
        return jax.random.normal(k, shape, f32) * scale

    def gain(k, shape):
        return 1.0 + 0.05 * jax.random.normal(k, shape, f32)

    return {
        'x': normal(ks[0], (BATCH, SEQ, D), 1.0),
        'p': normal(ks[1], (DEPTH, BATCH, SEQ, D_PLE), 1.0),
        'ffn1_norm': gain(ks[2], (L, D)),
        'w_ffn1_in': normal(ks[3], (L, D, 2 * D_FF), D ** -0.5),
        'w_ffn1_out': normal(ks[4], (L, D_FF, D), D_FF ** -0.5),
        'mix_norm': gain(ks[5], (L, D)),
        'w_mix_in': normal(ks[6], (L, D, IN_WIDTH), D ** -0.5),
        'sg_v_gain': gain(ks[7], (L, SG_WIDTH)),
        'sg_w': normal(ks[8], (L, SG_HEADS, SG_CHUNK, SG_CHUNK), 0.5 * SG_CHUNK ** -0.5),
        'sg_b': 1.0 + normal(ks[9], (L, SG_HEADS, SG_CHUNK), 0.1),
        'gla_w_gate': normal(ks[10], (L, GLA_GATE_RANK, GLA_KW), GLA_GATE_RANK ** -0.5),
        'gla_b_gate': normal(ks[11], (L, GLA_KW), 0.1),
        'gla_o_gain': gain(ks[12], (L, GLA_VW)),
        'w_mix_out': normal(ks[13], (L, MIX_WIDTH, D), MIX_WIDTH ** -0.5),
        'ffn2_norm': gain(ks[14], (L, D)),
        'w_ffn2_in': normal(ks[15], (L, D, 2 * D_FF), D ** -0.5),
        'w_ffn2_out': normal(ks[16], (L, D_FF, D), D_FF ** -0.5),
        'ple_norm': gain(ks[17], (L, D)),
        'w_ple_gate': normal(ks[18], (L, D, D), D ** -0.5),
        'w_ple_proj': normal(ks[19], (L, D_PLE, D), D_PLE ** -0.5),
        'final_norm': gain(ks[20], (D,)),
    }


def reference(x, p, ffn1_norm, w_ffn1_in, w_ffn1_out, mix_norm, w_mix_in, sg_v_gain, sg_w, sg_b,
              gla_w_gate, gla_b_gate, gla_o_gain, w_mix_out, ffn2_norm, w_ffn2_in, w_ffn2_out,
              ple_norm, w_ple_gate, w_ple_proj, final_norm):
    h = x
    for i in range(DEPTH):
        h = h + 0.5 * swiglu_ffn(rms_norm(h, ffn1_norm[i]), w_ffn1_in[i], w_ffn1_out[i])
        n = rms_norm(h, mix_norm[i])
        a_u, a_v, q, k, v, r, z = jnp.split(n @ w_mix_in[i], MIX_SPLITS, axis=-1)
        y_a = chunked_spatial_gating(jax.nn.gelu(a_u, approximate=False),
                                     jax.nn.gelu(a_v, approximate=False),
                                     sg_v_gain[i], sg_w[i], sg_b[i])
        y_b = gla_mixer(q, k, v, r, z, gla_w_gate[i], gla_b_gate[i], gla_o_gain[i])
        h = h + jnp.concatenate([y_a, y_b], axis=-1) @ w_mix_out[i]
        h = h + 0.5 * swiglu_ffn(rms_norm(h, ffn2_norm[i]), w_ffn2_in[i], w_ffn2_out[i])
        gate = jax.nn.sigmoid(rms_norm(h, ple_norm[i]) @ w_ple_gate[i])
        h = h + gate * (p[i] @ w_ple_proj[i])
    return rms_norm(h, final_norm)
```

```python
import functools

import jax
import jax.numpy as jnp
from jax import lax
from jax.experimental import pallas as pl
from jax.experimental.pallas import tpu as pltpu

F32 = jnp.float32
BF16 = jnp.bfloat16

EPS = 1e-6
D_MODEL = 2048
D_FF = 5632
D_PLE = 256

SG_HEADS = 8
SG_HEAD_DIM = 128
SG_WIDTH = SG_HEADS * SG_HEAD_DIM
SG_CHUNK = 128

GLA_HEADS = 4
GLA_DK = 128
GLA_DV = 256
GLA_KW = GLA_HEADS * GLA_DK
GLA_VW = GLA_HEADS * GLA_DV
GLA_GATE_RANK = 16
GLA_GATE_TAU = 16.0
GLA_CHUNK = 64
GLA_SUB = 16

MIX_MAIN = 2 * SG_WIDTH + 2 * GLA_KW + 2 * GLA_VW
OFF_Q = 2 * SG_WIDTH
OFF_K = OFF_Q + GLA_KW
OFF_V = OFF_K + GLA_KW
OFF_R = OFF_V + GLA_VW

VMEM_LIMIT = 56 * 1024 * 1024

FFN_TM = 512
FFN_TF = 512
MIX_TM = 512
MIX_TN = 1024
SG_TQ = 256
GLA_TILE = 512
OUT_TM = 512
OUT_TN = 1024
PLE_TM = 512


def _rms_norm_rows(h, g):
    ms = jnp.mean(h * h, axis=-1, keepdims=True)
    return h * lax.rsqrt(ms + EPS) * g


def _dot(a, b):
    return jnp.dot(a, b, preferred_element_type=F32)


def _dot_nt(a, b):
    return lax.dot_general(a, b, (((1,), (1,)), ((), ())), preferred_element_type=F32)


def _dot_tn(a, b):
    return lax.dot_general(a, b, (((0,), (0,)), ((), ())), preferred_element_type=F32)


def _ffn_kernel(h_ref, g_ref, wg_ref, wu_ref, wo_ref, o_ref, n_ref):
    @pl.when(pl.program_id(1) == 0)
    def _():
        h = h_ref[...]
        n_ref[...] = _rms_norm_rows(h, g_ref[...]).astype(BF16)
        o_ref[...] = h

    n = n_ref[...]
    gate = _dot(n, wg_ref[...])
    up = _dot(n, wu_ref[...])
    act = (0.5 * gate) * jax.nn.sigmoid(gate) * up
    o_ref[...] += _dot(act.astype(BF16), wo_ref[...])


def _ffn(h, g, w_in, w_out):
    n_rows = h.shape[0]
    tm, tf = FFN_TM, FFN_TF
    nf = D_FF // tf
    return pl.pallas_call(
        _ffn_kernel,
        grid=(n_rows // tm, nf),
        in_specs=[
            pl.BlockSpec((tm, D_MODEL), lambda i, j: (i, 0)),
            pl.BlockSpec((1, D_MODEL), lambda i, j: (0, 0)),
            pl.BlockSpec((D_MODEL, tf), lambda i, j: (0, j)),
            pl.BlockSpec((D_MODEL, tf), lambda i, j: (0, j + nf)),
            pl.BlockSpec((tf, D_MODEL), lambda i, j: (j, 0)),
        ],
        out_specs=pl.BlockSpec((tm, D_MODEL), lambda i, j: (i, 0)),
        out_shape=jax.ShapeDtypeStruct((n_rows, D_MODEL), F32),
        scratch_shapes=[pltpu.VMEM((tm, D_MODEL), BF16)],
        compiler_params=pltpu.CompilerParams(
            dimension_semantics=("parallel", "arbitrary"),
            vmem_limit_bytes=VMEM_LIMIT),
        name="ffn",
    )(h, g, w_in, w_in, w_out)


def _mixin_kernel(h_ref, g_ref, w_ref, wz_ref, o_ref, z_ref, n_ref):
    @pl.when(pl.program_id(1) == 0)
    def _():
        n = _rms_norm_rows(h_ref[...], g_ref[...]).astype(BF16)
        n_ref[...] = n
        z_ref[...] = _dot(n, wz_ref[...])

    o_ref[...] = _dot(n_ref[...], w_ref[...])


def _mixin(h, g, w_main, w_z):
    n_rows = h.shape[0]
    tm, tn = MIX_TM, MIX_TN
    return pl.pallas_call(
        _mixin_kernel,
        grid=(n_rows // tm, MIX_MAIN // tn),
        in_specs=[
            pl.BlockSpec((tm, D_MODEL), lambda i, j: (i, 0)),
            pl.BlockSpec((1, D_MODEL), lambda i, j: (0, 0)),
            pl.BlockSpec((D_MODEL, tn), lambda i, j: (0, j)),
            pl.BlockSpec((D_MODEL, GLA_GATE_RANK), lambda i, j: (0, 0)),
        ],
        out_specs=[
            pl.BlockSpec((tm, tn), lambda i, j: (i, j)),
            pl.BlockSpec((tm, GLA_GATE_RANK), lambda i, j: (i, 0)),
        ],
        out_shape=[
            jax.ShapeDtypeStruct((n_rows, MIX_MAIN), F32),
            jax.ShapeDtypeStruct((n_rows, GLA_GATE_RANK), F32),
        ],
        scratch_shapes=[pltpu.VMEM((tm, D_MODEL), BF16)],
        compiler_params=pltpu.CompilerParams(
            dimension_semantics=("parallel", "arbitrary"),
            vmem_limit_bytes=VMEM_LIMIT),
        name="mix_in",
    )(h, g, w_main, w_z)


def _gelu(x):
    return 0.5 * x * (1.0 + lax.erf(x * (2.0 ** -0.5)))


def _sg_kernel(u_ref, v_ref, gain_ref, w_ref, b_ref, o_ref):
    row = lax.broadcasted_iota(jnp.int32, (SG_CHUNK, SG_CHUNK), 0)
    col = lax.broadcasted_iota(jnp.int32, (SG_CHUNK, SG_CHUNK), 1)
    causal = row >= col
    for hd in range(SG_HEADS):
        cols = slice(hd * SG_HEAD_DIM, (hd + 1) * SG_HEAD_DIM)
        w = jnp.where(causal, w_ref[hd], 0.0).astype(BF16)
        bias = b_ref[hd]
        gain = gain_ref[:, cols]
        for c in range(SG_TQ // SG_CHUNK):
            rows = slice(c * SG_CHUNK, (c + 1) * SG_CHUNK)
            u = _gelu(u_ref[rows, cols])
            v = _gelu(v_ref[rows, cols])
            vc = v - jnp.mean(v, axis=-1, keepdims=True)
            var = jnp.mean(vc * vc, axis=-1, keepdims=True)
            vn = vc * lax.rsqrt(var + EPS) * gain
            mixed = _dot(w, vn.astype(BF16)) + bias
            o_ref[rows, cols] = (u * mixed).astype(o_ref.dtype)


def _spatial_gating(proj, v_gain, w_s, b_s):
    n_rows = proj.shape[0]
    tq = SG_TQ
    return pl.pallas_call(
        _sg_kernel,
        grid=(n_rows // tq,),
        in_specs=[
            pl.BlockSpec((tq, SG_WIDTH), lambda i: (i, 0)),
            pl.BlockSpec((tq, SG_WIDTH), lambda i: (i, 1)),
            pl.BlockSpec((1, SG_WIDTH), lambda i: (0, 0)),
            pl.BlockSpec((SG_HEADS, SG_CHUNK, SG_CHUNK), lambda i: (0, 0, 0)),
            pl.BlockSpec((SG_HEADS, SG_CHUNK, 1), lambda i: (0, 0, 0)),
        ],
        out_specs=pl.BlockSpec((tq, SG_WIDTH), lambda i: (i, 0)),
        out_shape=jax.ShapeDtypeStruct((n_rows, SG_WIDTH), BF16),
        compiler_params=pltpu.CompilerParams(
            dimension_semantics=("parallel",),
            vmem_limit_bytes=VMEM_LIMIT),
        name="spatial_gating",
    )(proj, proj, v_gain, w_s, b_s)


def _split3(x):
    x1 = x.astype(BF16)
    r1 = x - x1.astype(F32)
    x2 = r1.astype(BF16)
    x3 = (r1 - x2.astype(F32)).astype(BF16)
    return x1, x2, x3


def _gla_kernel(q_ref, k_ref, v_ref, r_ref, z_ref, wg_ref, bg_ref, og_ref, o_ref, st_ref):
    C = GLA_CHUNK

    @pl.when(pl.program_id(2) == 0)
    def _():
        st_ref[...] = jnp.zeros_like(st_ref)

    row = lax.broadcasted_iota(jnp.int32, (C, C), 0)
    col = lax.broadcasted_iota(jnp.int32, (C, C), 1)
    tril = jnp.where(row >= col, 1.0, 0.0).astype(BF16)
    half = C // 2
    quarter = C // 4
    m1 = (row >= half) & (col < half)
    m2 = ((row // quarter) % 2 == 1) & ((col // quarter) == (row // quarter) - 1)
    diag_off = col - (row // GLA_SUB) * GLA_SUB
    row_k = lax.broadcasted_iota(jnp.int32, (C, GLA_DK), 0)
    w_gate = wg_ref[...].astype(BF16)
    b_gate = bg_ref[...]
    o_gain = og_ref[...]

    def chunk(c, carry):
        rows = pl.ds(pl.multiple_of(c * C, C), C)
        q = q_ref[0, rows, :] * (GLA_DK ** -0.5)
        k = k_ref[0, rows, :]
        v = v_ref[0, rows, :]
        v16 = v.astype(BF16)

        pre = _dot(z_ref[0, rows, :].astype(BF16), w_gate) + b_gate
        g = (jnp.minimum(pre, 0.0) - jnp.log1p(jnp.exp(-jnp.abs(pre)))) * (1.0 / GLA_GATE_TAU)
        g1, g2, g3 = _split3(g)
        b = _dot(tril, g1) + _dot(tril, g2) + _dot(tril, g3)
        b_last = b[C - 1:C, :]

        st = st_ref[...]
        o = _dot_nt((q * jnp.exp(b)).astype(BF16), st.astype(BF16))

        ref1 = b[half - 1:half, :]
        q1 = q * jnp.exp(jnp.minimum(b - ref1, 0.0))
        k1 = k * jnp.exp(jnp.minimum(ref1 - b, 0.0))
        p1 = _dot_nt(q1.astype(BF16), k1.astype(BF16))
        ref2 = jnp.where(row_k < half, b[quarter - 1:quarter, :], b[half + quarter - 1:half + quarter, :])
        q2 = q * jnp.exp(jnp.minimum(b - ref2, 0.0))
        k2 = k * jnp.exp(jnp.minimum(ref2 - b, 0.0))
        p2 = _dot_nt(q2.astype(BF16), k2.astype(BF16))
        s = jnp.where(m1, p1, 0.0) + jnp.where(m2, p2, 0.0)

        nsub = C // GLA_SUB
        b4 = b.reshape(nsub, GLA_SUB, GLA_DK)
        k4 = k.reshape(nsub, GLA_SUB, GLA_DK)
        for j in range(GLA_SUB):
            bj = jnp.broadcast_to(b4[:, j:j + 1, :], b4.shape).reshape(C, GLA_DK)
            kj = jnp.broadcast_to(k4[:, j:j + 1, :], k4.shape).reshape(C, GLA_DK)
            e = jnp.exp(jnp.minimum(b - bj, 0.0))
            sj = jnp.sum(q * kj * e, axis=-1, keepdims=True)
            s = jnp.where(diag_off == j, sj, s)
        s = jnp.where(row >= col, s, 0.0)
        o = o + _dot(s.astype(BF16), v16)

        k_dec = k * jnp.exp(b_last - b)
        st_ref[...] = st * jnp.exp(b_last) + _dot_tn(v16, k_dec.astype(BF16))

        ms = jnp.mean(o * o, axis=-1, keepdims=True)
        r = r_ref[0, rows, :]
        y = o * lax.rsqrt(ms + EPS) * o_gain * (r * jax.nn.sigmoid(r))
        o_ref[0, rows, :] = y.astype(o_ref.dtype)
        return carry

    lax.fori_loop(0, GLA_TILE // C, chunk, 0)


def _gla(proj3, z3, w_gate, b_gate, o_gain):
    bsz, t_len, _ = proj3.shape
    tile = GLA_TILE
    return pl.pallas_call(
        _gla_kernel,
        grid=(bsz, GLA_HEADS, t_len // tile),
        in_specs=[
            pl.BlockSpec((1, tile, GLA_DK), lambda b, h, t: (b, t, OFF_Q // GLA_DK + h)),
            pl.BlockSpec((1, tile, GLA_DK), lambda b, h, t: (b, t, OFF_K // GLA_DK + h)),
            pl.BlockSpec((1, tile, GLA_DV), lambda b, h, t: (b, t, OFF_V // GLA_DV + h)),
            pl.BlockSpec((1, tile, GLA_DV), lambda b, h, t: (b, t, OFF_R // GLA_DV + h)),
            pl.BlockSpec((1, tile, GLA_GATE_RANK), lambda b, h, t: (b, t, 0)),
            pl.BlockSpec((GLA_GATE_RANK, GLA_DK), lambda b, h, t: (0, h)),
            pl.BlockSpec((1, GLA_DK), lambda b, h, t: (0, h)),
            pl.BlockSpec((1, GLA_DV), lambda b, h, t: (0, h)),
        ],
        out_specs=pl.BlockSpec((1, tile, GLA_DV), lambda b, h, t: (b, t, h)),
        out_shape=jax.ShapeDtypeStruct((bsz, t_len, GLA_VW), BF16),
        scratch_shapes=[pltpu.VMEM((GLA_DV, GLA_DK), F32)],
        compiler_params=pltpu.CompilerParams(
            dimension_semantics=("parallel", "parallel", "arbitrary"),
            vmem_limit_bytes=VMEM_LIMIT),
        name="gla",
    )(proj3, proj3, proj3, proj3, z3, w_gate, b_gate, o_gain)


def _mixout_kernel(h_ref, ya_ref, yb_ref, w_ref, o_ref):
    acc = _dot(ya_ref[...], w_ref[:SG_WIDTH, :])
    acc += _dot(yb_ref[...], w_ref[SG_WIDTH:, :])
    o_ref[...] = h_ref[...] + acc


def _mixout(h, ya, yb, w):
    n_rows = h.shape[0]
    tm, tn = OUT_TM, OUT_TN
    return pl.pallas_call(
        _mixout_kernel,
        grid=(n_rows // tm, D_MODEL // tn),
        in_specs=[
            pl.BlockSpec((tm, tn), lambda i, j: (i, j)),
            pl.BlockSpec((tm, SG_WIDTH), lambda i, j: (i, 0)),
            pl.BlockSpec((tm, GLA_VW), lambda i, j: (i, 0)),
            pl.BlockSpec((SG_WIDTH + GLA_VW, tn), lambda i, j: (0, j)),
        ],
        out_specs=pl.BlockSpec((tm, tn), lambda i, j: (i, j)),
        out_shape=jax.ShapeDtypeStruct((n_rows, D_MODEL), F32),
        compiler_params=pltpu.CompilerParams(
            dimension_semantics=("parallel", "parallel"),
            vmem_limit_bytes=VMEM_LIMIT),
        name="mix_out",
    )(h, ya, yb, w)


def _ple_kernel(h_ref, g_ref, wg_ref, p_ref, wp_ref, fg_ref, o_ref, *, final_norm):
    h = h_ref[...]
    n = _rms_norm_rows(h, g_ref[...]).astype(BF16)
    gate = jax.nn.sigmoid(_dot(n, wg_ref[...]))
    emb = _dot(p_ref[...].astype(BF16), wp_ref[...])
    out = h + gate * emb
    if final_norm:
        out = _rms_norm_rows(out, fg_ref[...])
    o_ref[...] = out


def _ple(h, g, w_gate, p, w_proj, final_g, final_norm):
    n_rows = h.shape[0]
    tm = PLE_TM
    const = pl.Buffered(1)
    return pl.pallas_call(
        functools.partial(_ple_kernel, final_norm=final_norm),
        grid=(n_rows // tm,),
        in_specs=[
            pl.BlockSpec((tm, D_MODEL), lambda i: (i, 0)),
            pl.BlockSpec((1, D_MODEL), lambda i: (0, 0)),
            pl.BlockSpec((D_MODEL, D_MODEL), lambda i: (0, 0), pipeline_mode=const),
            pl.BlockSpec((tm, D_PLE), lambda i: (i, 0)),
            pl.BlockSpec((D_PLE, D_MODEL), lambda i: (0, 0), pipeline_mode=const),
            pl.BlockSpec((1, D_MODEL), lambda i: (0, 0)),
        ],
        out_specs=pl.BlockSpec((tm, D_MODEL), lambda i: (i, 0)),
        out_shape=jax.ShapeDtypeStruct((n_rows, D_MODEL), F32),
        compiler_params=pltpu.CompilerParams(
            dimension_semantics=("parallel",),
            vmem_limit_bytes=VMEM_LIMIT),
        name="ple",
    )(h, g, w_gate, p, w_proj, final_g)


def kernel(x, p, ffn1_norm, w_ffn1_in, w_ffn1_out, mix_norm, w_mix_in, sg_v_gain, sg_w, sg_b,
           gla_w_gate, gla_b_gate, gla_o_gain, w_mix_out, ffn2_norm, w_ffn2_in, w_ffn2_out,
           ple_norm, w_ple_gate, w_ple_proj, final_norm):
    bsz, t_len, d = x.shape
    depth = p.shape[0]
    n_rows = bsz * t_len
    h = x.reshape(n_rows, d)
    final_g = final_norm.reshape(1, d)
    for i in range(depth):
        h = _ffn(h, ffn1_norm[i].reshape(1, d), w_ffn1_in[i].astype(BF16), w_ffn1_out[i].astype(BF16))

        w_in = w_mix_in[i]
        proj, z = _mixin(h, mix_norm[i].reshape(1, d), w_in[:, :MIX_MAIN].astype(BF16),
                         w_in[:, MIX_MAIN:].astype(BF16))
        y_a = _spatial_gating(proj, sg_v_gain[i].reshape(1, SG_WIDTH), sg_w[i],
                              sg_b[i].reshape(SG_HEADS, SG_CHUNK, 1))
        y_b = _gla(proj.reshape(bsz, t_len, MIX_MAIN), z.reshape(bsz, t_len, GLA_GATE_RANK),
                   gla_w_gate[i], gla_b_gate[i].reshape(1, GLA_KW), gla_o_gain[i].reshape(1, GLA_VW))
        h = _mixout(h, y_a, y_b.reshape(n_rows, GLA_VW), w_mix_out[i].astype(BF16))

        h = _ffn(h, ffn2_norm[i].reshape(1, d), w_ffn2_in[i].astype(BF16), w_ffn2_out[i].astype(BF16))

        h = _ple(h, ple_norm[i].reshape(1, d), w_ple_gate[i].astype(BF16), p[i].reshape(n_rows, D_PLE),
                 w_ple_proj[i].astype(BF16), final_g, final_norm=(i == depth - 1))
    return h.reshape(bsz, t_len, d)
```

```python
import functools
import math

import jax
import jax.numpy as jnp
from jax import lax
from jax.experimental import pallas as pl
from jax.experimental.pallas import tpu as pltpu

F32 = jnp.float32
BF16 = jnp.bfloat16

EPS = 1e-6
D_MODEL = 2048
D_FF = 5632
D_PLE = 256

SG_HEADS = 8
SG_HEAD_DIM = 128
SG_WIDTH = SG_HEADS * SG_HEAD_DIM
SG_CHUNK = 128

GLA_HEADS = 4
GLA_DK = 128
GLA_DV = 256
GLA_KW = GLA_HEADS * GLA_DK
GLA_VW = GLA_HEADS * GLA_DV
GLA_GATE_RANK = 16
GLA_GATE_TAU = 16.0
GLA_CHUNK = 64
GLA_SUB = 8

MIX_MAIN = 2 * SG_WIDTH + 2 * GLA_KW + 2 * GLA_VW
OFF_Q = 2 * SG_WIDTH
OFF_K = OFF_Q + GLA_KW
OFF_V = OFF_K + GLA_KW
OFF_R = OFF_V + GLA_VW

VMEM_LIMIT = 56 * 1024 * 1024

FFN_TM = 1024
FFN_TF = 512
MIX_TM = 1024
MIX_TN = 1024
SG_TQ = 256
GLA_TILE = 512
OUT_TM = 1024
OUT_TN = 1024
PLE_TM = 512


def _rms_norm_rows(h, g):
    ms = jnp.mean(h * h, axis=-1, keepdims=True)
    return h * lax.rsqrt(ms + EPS) * g


def _dot(a, b):
    return jnp.dot(a, b, preferred_element_type=F32)


def _dot_nt(a, b):
    return lax.dot_general(a, b, (((1,), (1,)), ((), ())), preferred_element_type=F32)


def _dot_tn(a, b):
    return lax.dot_general(a, b, (((0,), (0,)), ((), ())), preferred_element_type=F32)


def _ffn_kernel(h_ref, g_ref, wg_ref, wu_ref, wo_ref, o_ref, n_ref):
    @pl.when(pl.program_id(1) == 0)
    def _():
        h = h_ref[...]
        n_ref[...] = _rms_norm_rows(h, g_ref[...]).astype(BF16)
        o_ref[...] = h

    n = n_ref[...]
    gate = _dot(n, wg_ref[...])
    up = _dot(n, wu_ref[...])
    act = (0.5 * gate) * jax.nn.sigmoid(gate) * up
    o_ref[...] += _dot(act.astype(BF16), wo_ref[...])


def _ffn(h, g, w_in, w_out):
    n_rows = h.shape[0]
    tm, tf = FFN_TM, FFN_TF
    nf = D_FF // tf
    return pl.pallas_call(
        _ffn_kernel,
        grid=(n_rows // tm, nf),
        in_specs=[
            pl.BlockSpec((tm, D_MODEL), lambda i, j: (i, 0), pipeline_mode=pl.Buffered(1)),
            pl.BlockSpec((1, D_MODEL), lambda i, j: (0, 0)),
            pl.BlockSpec((D_MODEL, tf), lambda i, j: (0, j)),
            pl.BlockSpec((D_MODEL, tf), lambda i, j: (0, j + nf)),
            pl.BlockSpec((tf, D_MODEL), lambda i, j: (j, 0)),
        ],
        out_specs=pl.BlockSpec((tm, D_MODEL), lambda i, j: (i, 0)),
        out_shape=jax.ShapeDtypeStruct((n_rows, D_MODEL), F32),
        scratch_shapes=[pltpu.VMEM((tm, D_MODEL), BF16)],
        compiler_params=pltpu.CompilerParams(
            dimension_semantics=("parallel", "arbitrary"),
            vmem_limit_bytes=VMEM_LIMIT),
        name="ffn",
    )(h, g, w_in, w_in, w_out)


def _mixin_kernel(h_ref, g_ref, w_ref, wz_ref, o_ref, z_ref, n_ref):
    @pl.when(pl.program_id(1) == 0)
    def _():
        n = _rms_norm_rows(h_ref[...], g_ref[...]).astype(BF16)
        n_ref[...] = n
        z_ref[...] = _dot(n, wz_ref[...])

    o_ref[...] = _dot(n_ref[...], w_ref[...])


def _mixin(h, g, w_main, w_z):
    n_rows = h.shape[0]
    tm, tn = MIX_TM, MIX_TN
    return pl.pallas_call(
        _mixin_kernel,
        grid=(n_rows // tm, MIX_MAIN // tn),
        in_specs=[
            pl.BlockSpec((tm, D_MODEL), lambda i, j: (i, 0), pipeline_mode=pl.Buffered(1)),
            pl.BlockSpec((1, D_MODEL), lambda i, j: (0, 0)),
            pl.BlockSpec((D_MODEL, tn), lambda i, j: (0, j)),
            pl.BlockSpec((D_MODEL, GLA_GATE_RANK), lambda i, j: (0, 0)),
        ],
        out_specs=[
            pl.BlockSpec((tm, tn), lambda i, j: (i, j)),
            pl.BlockSpec((tm, GLA_GATE_RANK), lambda i, j: (i, 0)),
        ],
        out_shape=[
            jax.ShapeDtypeStruct((n_rows, MIX_MAIN), F32),
            jax.ShapeDtypeStruct((n_rows, GLA_GATE_RANK), F32),
        ],
        scratch_shapes=[pltpu.VMEM((tm, D_MODEL), BF16)],
        compiler_params=pltpu.CompilerParams(
            dimension_semantics=("parallel", "arbitrary"),
            vmem_limit_bytes=VMEM_LIMIT),
        name="mix_in",
    )(h, g, w_main, w_z)


def _gelu(x):
    return 0.5 * x * (1.0 + lax.erf(x * (2.0 ** -0.5)))


def _sg_kernel(u_ref, v_ref, gain_ref, w_ref, b_ref, o_ref):
    row = lax.broadcasted_iota(jnp.int32, (SG_CHUNK, SG_CHUNK), 0)
    col = lax.broadcasted_iota(jnp.int32, (SG_CHUNK, SG_CHUNK), 1)
    causal = row >= col
    for hd in range(SG_HEADS):
        cols = slice(hd * SG_HEAD_DIM, (hd + 1) * SG_HEAD_DIM)
        w = jnp.where(causal, w_ref[hd], 0.0).astype(BF16)
        bias = b_ref[hd]
        gain = gain_ref[:, cols]
        for c in range(SG_TQ // SG_CHUNK):
            rows = slice(c * SG_CHUNK, (c + 1) * SG_CHUNK)
            u = _gelu(u_ref[rows, cols])
            v = _gelu(v_ref[rows, cols])
            vc = v - jnp.mean(v, axis=-1, keepdims=True)
            var = jnp.mean(vc * vc, axis=-1, keepdims=True)
            vn = vc * lax.rsqrt(var + EPS) * gain
            mixed = _dot(w, vn.astype(BF16)) + bias
            o_ref[rows, cols] = (u * mixed).astype(o_ref.dtype)


def _spatial_gating(proj, v_gain, w_s, b_s):
    n_rows = proj.shape[0]
    tq = SG_TQ
    return pl.pallas_call(
        _sg_kernel,
        grid=(n_rows // tq,),
        in_specs=[
            pl.BlockSpec((tq, SG_WIDTH), lambda i: (i, 0)),
            pl.BlockSpec((tq, SG_WIDTH), lambda i: (i, 1)),
            pl.BlockSpec((1, SG_WIDTH), lambda i: (0, 0)),
            pl.BlockSpec((SG_HEADS, SG_CHUNK, SG_CHUNK), lambda i: (0, 0, 0)),
            pl.BlockSpec((SG_HEADS, SG_CHUNK, 1), lambda i: (0, 0, 0)),
        ],
        out_specs=pl.BlockSpec((tq, SG_WIDTH), lambda i: (i, 0)),
        out_shape=jax.ShapeDtypeStruct((n_rows, SG_WIDTH), BF16),
        compiler_params=pltpu.CompilerParams(
            dimension_semantics=("parallel",),
            vmem_limit_bytes=VMEM_LIMIT),
        name="spatial_gating",
    )(proj, proj, v_gain, w_s, b_s)


def _split3(x):
    x1 = x.astype(BF16)
    r1 = x - x1.astype(F32)
    x2 = r1.astype(BF16)
    x3 = (r1 - x2.astype(F32)).astype(BF16)
    return x1, x2, x3


def _group_row(x, group, idx):
    n, d = x.shape
    x3 = x.reshape(n // group, group, d)
    return jnp.broadcast_to(x3[:, idx:idx + 1, :], x3.shape).reshape(n, d)


def _gla_kernel(q_ref, k_ref, v_ref, r_ref, z_ref, wg_ref, bg_ref, og_ref, o_ref, st_ref):
    C = GLA_CHUNK

    @pl.when(pl.program_id(1) == 0)
    def _():
        st_ref[...] = jnp.zeros_like(st_ref)

    row = lax.broadcasted_iota(jnp.int32, (C, C), 0)
    col = lax.broadcasted_iota(jnp.int32, (C, C), 1)
    causal = row >= col
    tril = jnp.where(causal, 1.0, 0.0).astype(BF16)
    levels = []
    s = C // 2
    while s >= GLA_SUB:
        levels.append((s, ((row // s) % 2 == 1) & ((col // s) == (row // s) - 1)))
        s //= 2
    diag_off = col - (row // GLA_SUB) * GLA_SUB
    w_gate = wg_ref[...].astype(BF16)
    b_gate = bg_ref[...]
    g_scale = math.log2(math.e) / GLA_GATE_TAU

    def chunk(c, carry):
        rows = pl.ds(pl.multiple_of(c * C, C), C)
        pre = _dot(z_ref[0, rows, :].astype(BF16), w_gate) + b_gate
        g = (jnp.minimum(pre, 0.0) - jnp.log1p(jnp.exp(-jnp.abs(pre)))) * g_scale
        g1, g2, g3 = _split3(g)
        b_all = _dot(tril, g1) + _dot(tril, g2) + _dot(tril, g3)

        for hd in range(GLA_HEADS):
            kc = slice(hd * GLA_DK, (hd + 1) * GLA_DK)
            vc = slice(hd * GLA_DV, (hd + 1) * GLA_DV)
            b = b_all[:, kc]
            q = q_ref[0, rows, kc] * (GLA_DK ** -0.5)
            k = k_ref[0, rows, kc]
            v16 = v_ref[0, rows, vc].astype(BF16)
            b_last = b[C - 1:C, :]

            st = st_ref[hd]
            o = _dot_nt((q * jnp.exp2(b)).astype(BF16), st.astype(BF16))

            sc = jnp.zeros((C, C), F32)
            for s, mask in levels:
                d = b - _group_row(b, 2 * s, s - 1)
                qs = q * jnp.exp2(jnp.minimum(d, 0.0))
                ks = k * jnp.exp2(jnp.minimum(-d, 0.0))
                sc = jnp.where(mask, _dot_nt(qs.astype(BF16), ks.astype(BF16)), sc)

            for j in range(GLA_SUB):
                e = jnp.exp2(b - _group_row(b, GLA_SUB, j))
                sj = jnp.sum(q * _group_row(k, GLA_SUB, j) * e, axis=-1, keepdims=True)
                sc = jnp.where(diag_off == j, sj, sc)
            sc = jnp.where(causal, sc, 0.0)
            o = o + _dot(sc.astype(BF16), v16)

            k_dec = k * jnp.exp2(b_last - b)
            st_ref[hd] = st * jnp.exp2(b_last) + _dot_tn(v16, k_dec.astype(BF16))

            ms = jnp.mean(o * o, axis=-1, keepdims=True)
            r = r_ref[0, rows, vc]
            y = o * lax.rsqrt(ms + EPS) * og_ref[:, vc] * (r * jax.nn.sigmoid(r))
            o_ref[0, rows, vc] = y.astype(o_ref.dtype)
        return carry

    lax.fori_loop(0, GLA_TILE // C, chunk, 0)


def _gla(proj3, z3, w_gate, b_gate, o_gain):
    bsz, t_len, _ = proj3.shape
    tile = GLA_TILE
    return pl.pallas_call(
        _gla_kernel,
        grid=(bsz, t_len // tile),
        in_specs=[
            pl.BlockSpec((1, tile, GLA_KW), lambda b, t: (b, t, OFF_Q // GLA_KW)),
            pl.BlockSpec((1, tile, GLA_KW), lambda b, t: (b, t, OFF_K // GLA_KW)),
            pl.BlockSpec((1, tile, GLA_VW), lambda b, t: (b, t, OFF_V // GLA_VW)),
            pl.BlockSpec((1, tile, GLA_VW), lambda b, t: (b, t, OFF_R // GLA_VW)),
            pl.BlockSpec((1, tile, GLA_GATE_RANK), lambda b, t: (b, t, 0)),
            pl.BlockSpec((GLA_GATE_RANK, GLA_KW), lambda b, t: (0, 0)),
            pl.BlockSpec((1, GLA_KW), lambda b, t: (0, 0)),
            pl.BlockSpec((1, GLA_VW), lambda b, t: (0, 0)),
        ],
        out_specs=pl.BlockSpec((1, tile, GLA_VW), lambda b, t: (b, t, 0)),
        out_shape=jax.ShapeDtypeStruct((bsz, t_len, GLA_VW), BF16),
        scratch_shapes=[pltpu.VMEM((GLA_HEADS, GLA_DV, GLA_DK), F32)],
        compiler_params=pltpu.CompilerParams(
            dimension_semantics=("parallel", "arbitrary"),
            vmem_limit_bytes=VMEM_LIMIT),
        name="gla",
    )(proj3, proj3, proj3, proj3, z3, w_gate, b_gate, o_gain)


def _mixout_kernel(h_ref, ya_ref, yb_ref, w_ref, o_ref):
    acc = _dot(ya_ref[...], w_ref[:SG_WIDTH, :])
    acc += _dot(yb_ref[...], w_ref[SG_WIDTH:, :])
    o_ref[...] = h_ref[...] + acc


def _mixout(h, ya, yb, w):
    n_rows = h.shape[0]
    tm, tn = OUT_TM, OUT_TN
    return pl.pallas_call(
        _mixout_kernel,
        grid=(n_rows // tm, D_MODEL // tn),
        in_specs=[
            pl.BlockSpec((tm, tn), lambda i, j: (i, j)),
            pl.BlockSpec((tm, SG_WIDTH), lambda i, j: (i, 0)),
            pl.BlockSpec((tm, GLA_VW), lambda i, j: (i, 0)),
            pl.BlockSpec((SG_WIDTH + GLA_VW, tn), lambda i, j: (0, j)),
        ],
        out_specs=pl.BlockSpec((tm, tn), lambda i, j: (i, j)),
        out_shape=jax.ShapeDtypeStruct((n_rows, D_MODEL), F32),
        compiler_params=pltpu.CompilerParams(
            dimension_semantics=("parallel", "parallel"),
            vmem_limit_bytes=VMEM_LIMIT),
        name="mix_out",
    )(h, ya, yb, w)


def _ple_kernel(h_ref, g_ref, wg_ref, p_ref, wp_ref, fg_ref, o_ref, *, final_norm):
    h = h_ref[...]
    n = _rms_norm_rows(h, g_ref[...]).astype(BF16)
    gate = jax.nn.sigmoid(_dot(n, wg_ref[...]))
    emb = _dot(p_ref[...].astype(BF16), wp_ref[...])
    out = h + gate * emb
    if final_norm:
        out = _rms_norm_rows(out, fg_ref[...])
    o_ref[...] = out


def _ple(h, g, w_gate, p, w_proj, final_g, final_norm):
    n_rows = h.shape[0]
    tm = PLE_TM
    const = pl.Buffered(1)
    return pl.pallas_call(
        functools.partial(_ple_kernel, final_norm=final_norm),
        grid=(n_rows // tm,),
        in_specs=[
            pl.BlockSpec((tm, D_MODEL), lambda i: (i, 0)),
            pl.BlockSpec((1, D_MODEL), lambda i: (0, 0)),
            pl.BlockSpec((D_MODEL, D_MODEL), lambda i: (0, 0), pipeline_mode=const),
            pl.BlockSpec((tm, D_PLE), lambda i: (i, 0)),
            pl.BlockSpec((D_PLE, D_MODEL), lambda i: (0, 0), pipeline_mode=const),
            pl.BlockSpec((1, D_MODEL), lambda i: (0, 0)),
        ],
        out_specs=pl.BlockSpec((tm, D_MODEL), lambda i: (i, 0)),
        out_shape=jax.ShapeDtypeStruct((n_rows, D_MODEL), F32),
        compiler_params=pltpu.CompilerParams(
            dimension_semantics=("parallel",),
            vmem_limit_bytes=VMEM_LIMIT),
        name="ple",
    )(h, g, w_gate, p, w_proj, final_g)


def kernel(x, p, ffn1_norm, w_ffn1_in, w_ffn1_out, mix_norm, w_mix_in, sg_v_gain, sg_w, sg_b,
           gla_w_gate, gla_b_gate, gla_o_gain, w_mix_out, ffn2_norm, w_ffn2_in, w_ffn2_out,
           ple_norm, w_ple_gate, w_ple_proj, final_norm):
    bsz, t_len, d = x.shape
    depth = p.shape[0]
    n_rows = bsz * t_len
    h = x.reshape(n_rows, d)
    final_g = final_norm.reshape(1, d)
    for i in range(depth):
        h = _ffn(h, ffn1_norm[i].reshape(1, d), w_ffn1_in[i].astype(BF16), w_ffn1_out[i].astype(BF16))

        w_in = w_mix_in[i]
        proj, z = _mixin(h, mix_norm[i].reshape(1, d), w_in[:, :MIX_MAIN].astype(BF16),
                         w_in[:, MIX_MAIN:].astype(BF16))
        y_a = _spatial_gating(proj, sg_v_gain[i].reshape(1, SG_WIDTH), sg_w[i],
                              sg_b[i].reshape(SG_HEADS, SG_CHUNK, 1))
        y_b = _gla(proj.reshape(bsz, t_len, MIX_MAIN), z.reshape(bsz, t_len, GLA_GATE_RANK),
                   gla_w_gate[i], gla_b_gate[i].reshape(1, GLA_KW), gla_o_gain[i].reshape(1, GLA_VW))
        h = _mixout(h, y_a, y_b.reshape(n_rows, GLA_VW), w_mix_out[i].astype(BF16))

        h = _ffn(h, ffn2_norm[i].reshape(1, d), w_ffn2_in[i].astype(BF16), w_ffn2_out[i].astype(BF16))

        h = _ple(h, ple_norm[i].reshape(1, d), w_ple_gate[i].astype(BF16), p[i].reshape(n_rows, D_PLE),
                 w_ple_proj[i].astype(BF16), final_g, final_norm=(i == depth - 1))
    return h.reshape(bsz, t_len, d)
```

```python
import functools
import math

import jax
import jax.numpy as jnp
from jax import lax
from jax.experimental import pallas as pl
from jax.experimental.pallas import tpu as pltpu

F32 = jnp.float32
BF16 = jnp.bfloat16

EPS = 1e-6
D_MODEL = 2048
D_FF = 5632
D_PLE = 256

SG_HEADS = 8
SG_HEAD_DIM = 128
SG_WIDTH = SG_HEADS * SG_HEAD_DIM
SG_CHUNK = 128

GLA_HEADS = 4
GLA_DK = 128
GLA_DV = 256
GLA_KW = GLA_HEADS * GLA_DK
GLA_VW = GLA_HEADS * GLA_DV
GLA_GATE_RANK = 16
GLA_GATE_TAU = 16.0
GLA_CHUNK = 64
GLA_SUB = 8

MIX_MAIN = 2 * SG_WIDTH + 2 * GLA_KW + 2 * GLA_VW
OFF_Q = 2 * SG_WIDTH
OFF_K = OFF_Q + GLA_KW
OFF_V = OFF_K + GLA_KW
OFF_R = OFF_V + GLA_VW

VMEM_LIMIT = 56 * 1024 * 1024
LANES = 128

FFN_TM = 1024
FFN_TF = 512
MIX_TM = 1024
MIX_TN = 1024
SG_TQ = 256
GLA_TILE = 512
OUT_TM = 1024
OUT_TN = 1024
PLE_TM = 512


def _rms_norm_rows(h, g):
    ms = jnp.mean(h * h, axis=-1, keepdims=True)
    return h * lax.rsqrt(ms + EPS) * g


def _dot(a, b):
    return jnp.dot(a, b, preferred_element_type=F32)


def _dot_nt(a, b):
    return lax.dot_general(a, b, (((1,), (1,)), ((), ())), preferred_element_type=F32)


def _dot_tn(a, b):
    return lax.dot_general(a, b, (((0,), (0,)), ((), ())), preferred_element_type=F32)


def _ffn_kernel(h_ref, g_ref, wg_ref, wu_ref, wo_ref, o_ref, n_ref):
    @pl.when(pl.program_id(1) == 0)
    def _():
        h = h_ref[...]
        n_ref[...] = _rms_norm_rows(h, g_ref[...]).astype(BF16)
        o_ref[...] = h

    n = n_ref[...]
    gate = _dot(n, wg_ref[...])
    up = _dot(n, wu_ref[...])
    act = (0.5 * gate) * jax.nn.sigmoid(gate) * up
    o_ref[...] += _dot(act.astype(BF16), wo_ref[...])


def _ffn(h, g, w_in, w_out, layer):
    n_rows = h.shape[0]
    tm, tf = FFN_TM, FFN_TF
    nf = D_FF // tf
    return pl.pallas_call(
        _ffn_kernel,
        grid=(n_rows // tm, nf),
        in_specs=[
            pl.BlockSpec((tm, D_MODEL), lambda i, j: (i, 0)),
            pl.BlockSpec((None, 1, D_MODEL), lambda i, j: (layer, 0, 0)),
            pl.BlockSpec((None, D_MODEL, tf), lambda i, j: (layer, 0, j)),
            pl.BlockSpec((None, D_MODEL, tf), lambda i, j: (layer, 0, j + nf)),
            pl.BlockSpec((None, tf, D_MODEL), lambda i, j: (layer, j, 0)),
        ],
        out_specs=pl.BlockSpec((tm, D_MODEL), lambda i, j: (i, 0)),
        out_shape=jax.ShapeDtypeStruct((n_rows, D_MODEL), F32),
        scratch_shapes=[pltpu.VMEM((tm, D_MODEL), BF16)],
        compiler_params=pltpu.CompilerParams(
            dimension_semantics=("parallel", "arbitrary"),
            vmem_limit_bytes=VMEM_LIMIT),
        name="ffn",
    )(h, g, w_in, w_in, w_out)


def _mixin_kernel(h_ref, g_ref, w_ref, wz_ref, o_ref, z_ref, n_ref):
    @pl.when(pl.program_id(1) == 0)
    def _():
        n = _rms_norm_rows(h_ref[...], g_ref[...]).astype(BF16)
        n_ref[...] = n
        z_ref[...] = _dot(n, wz_ref[:, :GLA_GATE_RANK])

    o_ref[...] = _dot(n_ref[...], w_ref[...])


def _mixin(h, g, w, layer):
    n_rows = h.shape[0]
    tm, tn = MIX_TM, MIX_TN
    return pl.pallas_call(
        _mixin_kernel,
        grid=(n_rows // tm, MIX_MAIN // tn),
        in_specs=[
            pl.BlockSpec((tm, D_MODEL), lambda i, j: (i, 0)),
            pl.BlockSpec((None, 1, D_MODEL), lambda i, j: (layer, 0, 0)),
            pl.BlockSpec((None, D_MODEL, tn), lambda i, j: (layer, 0, j)),
            pl.BlockSpec((None, D_MODEL, LANES), lambda i, j: (layer, 0, MIX_MAIN // LANES)),
        ],
        out_specs=[
            pl.BlockSpec((tm, tn), lambda i, j: (i, j)),
            pl.BlockSpec((tm, GLA_GATE_RANK), lambda i, j: (i, 0)),
        ],
        out_shape=[
            jax.ShapeDtypeStruct((n_rows, MIX_MAIN), F32),
            jax.ShapeDtypeStruct((n_rows, GLA_GATE_RANK), F32),
        ],
        scratch_shapes=[pltpu.VMEM((tm, D_MODEL), BF16)],
        compiler_params=pltpu.CompilerParams(
            dimension_semantics=("parallel", "arbitrary"),
            vmem_limit_bytes=VMEM_LIMIT),
        name="mix_in",
    )(h, g, w, w)


def _gelu(x):
    return 0.5 * x * (1.0 + lax.erf(x * (2.0 ** -0.5)))


def _sg_kernel(u_ref, v_ref, gain_ref, w_ref, b_ref, o_ref):
    row = lax.broadcasted_iota(jnp.int32, (SG_CHUNK, SG_CHUNK), 0)
    col = lax.broadcasted_iota(jnp.int32, (SG_CHUNK, SG_CHUNK), 1)
    causal = row >= col
    for hd in range(SG_HEADS):
        cols = slice(hd * SG_HEAD_DIM, (hd + 1) * SG_HEAD_DIM)
        w = jnp.where(causal, w_ref[hd], 0.0).astype(BF16)
        bias = b_ref[hd]
        gain = gain_ref[:, cols]
        for c in range(SG_TQ // SG_CHUNK):
            rows = slice(c * SG_CHUNK, (c + 1) * SG_CHUNK)
            u = _gelu(u_ref[rows, cols])
            v = _gelu(v_ref[rows, cols])
            vc = v - jnp.mean(v, axis=-1, keepdims=True)
            var = jnp.mean(vc * vc, axis=-1, keepdims=True)
            vn = vc * lax.rsqrt(var + EPS) * gain
            mixed = _dot(w, vn.astype(BF16)) + bias
            o_ref[rows, cols] = (u * mixed).astype(o_ref.dtype)


def _spatial_gating(proj, v_gain, w_s, b_s, layer):
    n_rows = proj.shape[0]
    tq = SG_TQ
    return pl.pallas_call(
        _sg_kernel,
        grid=(n_rows // tq,),
        in_specs=[
            pl.BlockSpec((tq, SG_WIDTH), lambda i: (i, 0)),
            pl.BlockSpec((tq, SG_WIDTH), lambda i: (i, 1)),
            pl.BlockSpec((None, 1, SG_WIDTH), lambda i: (layer, 0, 0)),
            pl.BlockSpec((None, SG_HEADS, SG_CHUNK, SG_CHUNK), lambda i: (layer, 0, 0, 0)),
            pl.BlockSpec((None, SG_HEADS, SG_CHUNK, 1), lambda i: (layer, 0, 0, 0)),
        ],
        out_specs=pl.BlockSpec((tq, SG_WIDTH), lambda i: (i, 0)),
        out_shape=jax.ShapeDtypeStruct((n_rows, SG_WIDTH), BF16),
        compiler_params=pltpu.CompilerParams(
            dimension_semantics=("parallel",),
            vmem_limit_bytes=VMEM_LIMIT),
        name="spatial_gating",
    )(proj, proj, v_gain, w_s, b_s)


def _split3(x):
    x1 = x.astype(BF16)
    r1 = x - x1.astype(F32)
    x2 = r1.astype(BF16)
    x3 = (r1 - x2.astype(F32)).astype(BF16)
    return x1, x2, x3


def _group_row(x, group, idx):
    n, d = x.shape
    x3 = x.reshape(n // group, group, d)
    return jnp.broadcast_to(x3[:, idx:idx + 1, :], x3.shape).reshape(n, d)


def _gla_kernel(q_ref, k_ref, v_ref, r_ref, z_ref, wg_ref, bg_ref, og_ref, o_ref, st_ref):
    C = GLA_CHUNK

    @pl.when(pl.program_id(1) == 0)
    def _():
        st_ref[...] = jnp.zeros_like(st_ref)

    row = lax.broadcasted_iota(jnp.int32, (C, C), 0)
    col = lax.broadcasted_iota(jnp.int32, (C, C), 1)
    causal = row >= col
    tril = jnp.where(causal, 1.0, 0.0).astype(BF16)
    levels = []
    s = C // 2
    while s >= GLA_SUB:
        levels.append((s, ((row // s) % 2 == 1) & ((col // s) == (row // s) - 1)))
        s //= 2
    diag_off = col - (row // GLA_SUB) * GLA_SUB
    w_gate = wg_ref[...].astype(BF16)
    b_gate = bg_ref[...]
    g_scale = math.log2(math.e) / GLA_GATE_TAU

    def chunk(c, carry):
        rows = pl.ds(pl.multiple_of(c * C, C), C)
        pre = _dot(z_ref[0, rows, :].astype(BF16), w_gate) + b_gate
        g = (jnp.minimum(pre, 0.0) - jnp.log1p(jnp.exp(-jnp.abs(pre)))) * g_scale
        g1, g2, g3 = _split3(g)
        b_all = _dot(tril, g1) + _dot(tril, g2) + _dot(tril, g3)

        for hd in range(GLA_HEADS):
            kc = slice(hd * GLA_DK, (hd + 1) * GLA_DK)
            vc = slice(hd * GLA_DV, (hd + 1) * GLA_DV)
            b = b_all[:, kc]
            q = q_ref[0, rows, kc] * (GLA_DK ** -0.5)
            k = k_ref[0, rows, kc]
            v16 = v_ref[0, rows, vc].astype(BF16)
            b_last = b[C - 1:C, :]

            st = st_ref[hd]
            o = _dot_nt((q * jnp.exp2(b)).astype(BF16), st.astype(BF16))

            sc = jnp.zeros((C, C), F32)
            for s, mask in levels:
                d = b - _group_row(b, 2 * s, s - 1)
                qs = q * jnp.exp2(jnp.minimum(d, 0.0))
                ks = k * jnp.exp2(jnp.minimum(-d, 0.0))
                sc = jnp.where(mask, _dot_nt(qs.astype(BF16), ks.astype(BF16)), sc)

            for j in range(GLA_SUB):
                e = jnp.exp2(b - _group_row(b, GLA_SUB, j))
                sj = jnp.sum(q * _group_row(k, GLA_SUB, j) * e, axis=-1, keepdims=True)
                sc = jnp.where(diag_off == j, sj, sc)
            sc = jnp.where(causal, sc, 0.0)
            o = o + _dot(sc.astype(BF16), v16)

            k_dec = k * jnp.exp2(b_last - b)
            st_ref[hd] = st * jnp.exp2(b_last) + _dot_tn(v16, k_dec.astype(BF16))

            ms = jnp.mean(o * o, axis=-1, keepdims=True)
            r = r_ref[0, rows, vc]
            y = o * lax.rsqrt(ms + EPS) * og_ref[:, vc] * (r * jax.nn.sigmoid(r))
            o_ref[0, rows, vc] = y.astype(o_ref.dtype)
        return carry

    lax.fori_loop(0, GLA_TILE // C, chunk, 0)


def _gla(proj3, z3, w_gate, b_gate, o_gain, layer):
    bsz, t_len, _ = proj3.shape
    tile = GLA_TILE
    return pl.pallas_call(
        _gla_kernel,
        grid=(bsz, t_len // tile),
        in_specs=[
            pl.BlockSpec((1, tile, GLA_KW), lambda b, t: (b, t, OFF_Q // GLA_KW)),
            pl.BlockSpec((1, tile, GLA_KW), lambda b, t: (b, t, OFF_K // GLA_KW)),
            pl.BlockSpec((1, tile, GLA_VW), lambda b, t: (b, t, OFF_V // GLA_VW)),
            pl.BlockSpec((1, tile, GLA_VW), lambda b, t: (b, t, OFF_R // GLA_VW)),
            pl.BlockSpec((1, tile, GLA_GATE_RANK), lambda b, t: (b, t, 0)),
            pl.BlockSpec((None, GLA_GATE_RANK, GLA_KW), lambda b, t: (layer, 0, 0)),
            pl.BlockSpec((None, 1, GLA_KW), lambda b, t: (layer, 0, 0)),
            pl.BlockSpec((None, 1, GLA_VW), lambda b, t: (layer, 0, 0)),
        ],
        out_specs=pl.BlockSpec((1, tile, GLA_VW), lambda b, t: (b, t, 0)),
        out_shape=jax.ShapeDtypeStruct((bsz, t_len, GLA_VW), BF16),
        scratch_shapes=[pltpu.VMEM((GLA_HEADS, GLA_DV, GLA_DK), F32)],
        compiler_params=pltpu.CompilerParams(
            dimension_semantics=("parallel", "arbitrary"),
            vmem_limit_bytes=VMEM_LIMIT),
        name="gla",
    )(proj3, proj3, proj3, proj3, z3, w_gate, b_gate, o_gain)


def _mixout_kernel(h_ref, ya_ref, yb_ref, w_ref, o_ref):
    acc = _dot(ya_ref[...], w_ref[:SG_WIDTH, :])
    acc += _dot(yb_ref[...], w_ref[SG_WIDTH:, :])
    o_ref[...] = h_ref[...] + acc


def _mixout(h, ya, yb, w, layer):
    n_rows = h.shape[0]
    tm, tn = OUT_TM, OUT_TN
    return pl.pallas_call(
        _mixout_kernel,
        grid=(n_rows // tm, D_MODEL // tn),
        in_specs=[
            pl.BlockSpec((tm, tn), lambda i, j: (i, j)),
            pl.BlockSpec((tm, SG_WIDTH), lambda i, j: (i, 0)),
            pl.BlockSpec((tm, GLA_VW), lambda i, j: (i, 0)),
            pl.BlockSpec((None, SG_WIDTH + GLA_VW, tn), lambda i, j: (layer, 0, j)),
        ],
        out_specs=pl.BlockSpec((tm, tn), lambda i, j: (i, j)),
        out_shape=jax.ShapeDtypeStruct((n_rows, D_MODEL), F32),
        compiler_params=pltpu.CompilerParams(
            dimension_semantics=("parallel", "parallel"),
            vmem_limit_bytes=VMEM_LIMIT),
        name="mix_out",
    )(h, ya, yb, w)


def _ple_kernel(h_ref, g_ref, wg_ref, p_ref, wp_ref, fg_ref, o_ref, *, final_norm):
    h = h_ref[...]
    n = _rms_norm_rows(h, g_ref[...]).astype(BF16)
    gate = jax.nn.sigmoid(_dot(n, wg_ref[...]))
    emb = _dot(p_ref[...].astype(BF16), wp_ref[...])
    out = h + gate * emb
    if final_norm:
        out = _rms_norm_rows(out, fg_ref[...])
    o_ref[...] = out


def _ple(h, g, w_gate, p, w_proj, final_g, layer, final_norm):
    n_rows = h.shape[0]
    tm = PLE_TM
    const = pl.Buffered(1)
    return pl.pallas_call(
        functools.partial(_ple_kernel, final_norm=final_norm),
        grid=(n_rows // tm,),
        in_specs=[
            pl.BlockSpec((tm, D_MODEL), lambda i: (i, 0)),
            pl.BlockSpec((None, 1, D_MODEL), lambda i: (layer, 0, 0)),
            pl.BlockSpec((None, D_MODEL, D_MODEL), lambda i: (layer, 0, 0), pipeline_mode=const),
            pl.BlockSpec((None, tm, D_PLE), lambda i: (layer, i, 0)),
            pl.BlockSpec((None, D_PLE, D_MODEL), lambda i: (layer, 0, 0), pipeline_mode=const),
            pl.BlockSpec((1, D_MODEL), lambda i: (0, 0)),
        ],
        out_specs=pl.BlockSpec((tm, D_MODEL), lambda i: (i, 0)),
        out_shape=jax.ShapeDtypeStruct((n_rows, D_MODEL), F32),
        compiler_params=pltpu.CompilerParams(
            dimension_semantics=("parallel",),
            vmem_limit_bytes=VMEM_LIMIT),
        name="ple",
    )(h, g, w_gate, p, w_proj, final_g)


def kernel(x, p, ffn1_norm, w_ffn1_in, w_ffn1_out, mix_norm, w_mix_in, sg_v_gain, sg_w, sg_b,
           gla_w_gate, gla_b_gate, gla_o_gain, w_mix_out, ffn2_norm, w_ffn2_in, w_ffn2_out,
           ple_norm, w_ple_gate, w_ple_proj, final_norm):
    bsz, t_len, d = x.shape
    depth = p.shape[0]
    n_rows = bsz * t_len
    h = x.reshape(n_rows, d)

    def row(v):
        return v.reshape(depth, 1, v.shape[-1])

    ffn1_g, mix_g, ffn2_g, ple_g = row(ffn1_norm), row(mix_norm), row(ffn2_norm), row(ple_norm)
    w1_in, w1_out = w_ffn1_in.astype(BF16), w_ffn1_out.astype(BF16)
    w2_in, w2_out = w_ffn2_in.astype(BF16), w_ffn2_out.astype(BF16)
    w_mi, w_mo = w_mix_in.astype(BF16), w_mix_out.astype(BF16)
    w_pg, w_pp = w_ple_gate.astype(BF16), w_ple_proj.astype(BF16)
    p2 = p.reshape(depth, n_rows, D_PLE)
    sg_gain, sg_bias = row(sg_v_gain), sg_b.reshape(depth, SG_HEADS, SG_CHUNK, 1)
    gla_bg, gla_og = row(gla_b_gate), row(gla_o_gain)
    final_g = final_norm.reshape(1, d)

    for i in range(depth):
        h = _ffn(h, ffn1_g, w1_in, w1_out, i)
        proj, z = _mixin(h, mix_g, w_mi, i)
        y_a = _spatial_gating(proj, sg_gain, sg_w, sg_bias, i)
        y_b = _gla(proj.reshape(bsz, t_len, MIX_MAIN), z.reshape(bsz, t_len, GLA_GATE_RANK),
                   gla_w_gate, gla_bg, gla_og, i)
        h = _mixout(h, y_a, y_b.reshape(n_rows, GLA_VW), w_mo, i)
        h = _ffn(h, ffn2_g, w2_in, w2_out, i)
        h = _ple(h, ple_g, w_pg, p2, w_pp, final_g, i, final_norm=(i == depth - 1))
    return h.reshape(bsz, t_len, d)
```

```python
import functools
import math

import jax
import jax.numpy as jnp
from jax import lax
from jax.experimental import pallas as pl
from jax.experimental.pallas import tpu as pltpu

F32 = jnp.float32
BF16 = jnp.bfloat16

EPS = 1e-6
D_MODEL = 2048
D_FF = 5632
D_PLE = 256

SG_HEADS = 8
SG_HEAD_DIM = 128
SG_WIDTH = SG_HEADS * SG_HEAD_DIM
SG_CHUNK = 128

GLA_HEADS = 4
GLA_DK = 128
GLA_DV = 256
GLA_KW = GLA_HEADS * GLA_DK
GLA_VW = GLA_HEADS * GLA_DV
GLA_GATE_RANK = 16
GLA_GATE_TAU = 16.0
GLA_CHUNK = 64
GLA_SUB = 8

MIX_MAIN = 2 * SG_WIDTH + 2 * GLA_KW + 2 * GLA_VW
OFF_Q = 2 * SG_WIDTH
OFF_K = OFF_Q + GLA_KW
OFF_V = OFF_K + GLA_KW
OFF_R = OFF_V + GLA_VW

VMEM_LIMIT = 56 * 1024 * 1024
LANES = 128

FFN_TM = 1024
FFN_TF = 256
MIX_TM = 1024
MIX_TN = 1024
SG_TQ = 256
GLA_TILE = 512
OUT_TM = 512
PLE_TM = 256


def _rms_norm_rows(h, g):
    ms = jnp.mean(h * h, axis=-1, keepdims=True)
    return h * lax.rsqrt(ms + EPS) * g


def _dot(a, b):
    return jnp.dot(a, b, preferred_element_type=F32)


def _dot_nt(a, b):
    return lax.dot_general(a, b, (((1,), (1,)), ((), ())), preferred_element_type=F32)


def _dot_tn(a, b):
    return lax.dot_general(a, b, (((0,), (0,)), ((), ())), preferred_element_type=F32)


def _ffn_kernel(h_ref, g_ref, wg_ref, wu_ref, wo_ref, o_ref, n_ref):
    @pl.when(pl.program_id(1) == 0)
    def _():
        h = h_ref[...]
        n_ref[...] = _rms_norm_rows(h, g_ref[...]).astype(BF16)
        o_ref[...] = h

    n = n_ref[...]
    gate = _dot(n, wg_ref[...].astype(BF16))
    up = _dot(n, wu_ref[...].astype(BF16))
    act = (0.5 * gate) * jax.nn.sigmoid(gate) * up
    o_ref[...] += _dot(act.astype(BF16), wo_ref[...].astype(BF16))


def _ffn(h, g, w_in, w_out, layer):
    n_rows = h.shape[0]
    tm, tf = FFN_TM, FFN_TF
    nf = D_FF // tf
    return pl.pallas_call(
        _ffn_kernel,
        grid=(n_rows // tm, nf),
        in_specs=[
            pl.BlockSpec((tm, D_MODEL), lambda i, j: (i, 0)),
            pl.BlockSpec((None, 1, D_MODEL), lambda i, j: (layer, 0, 0)),
            pl.BlockSpec((None, D_MODEL, tf), lambda i, j: (layer, 0, j)),
            pl.BlockSpec((None, D_MODEL, tf), lambda i, j: (layer, 0, j + nf)),
            pl.BlockSpec((None, tf, D_MODEL), lambda i, j: (layer, j, 0)),
        ],
        out_specs=pl.BlockSpec((tm, D_MODEL), lambda i, j: (i, 0)),
        out_shape=jax.ShapeDtypeStruct((n_rows, D_MODEL), F32),
        scratch_shapes=[pltpu.VMEM((tm, D_MODEL), BF16)],
        compiler_params=pltpu.CompilerParams(
            dimension_semantics=("parallel", "arbitrary"),
            vmem_limit_bytes=VMEM_LIMIT),
        name="ffn",
    )(h, g, w_in, w_in, w_out)


def _mixin_kernel(h_ref, g_ref, w_ref, wz_ref, o_ref, z_ref, n_ref):
    @pl.when(pl.program_id(1) == 0)
    def _():
        n = _rms_norm_rows(h_ref[...], g_ref[...]).astype(BF16)
        n_ref[...] = n
        z_ref[...] = _dot(n, wz_ref[:, :GLA_GATE_RANK].astype(BF16))

    o_ref[...] = _dot(n_ref[...], w_ref[...].astype(BF16))


def _mixin(h, g, w, layer):
    n_rows = h.shape[0]
    tm, tn = MIX_TM, MIX_TN
    return pl.pallas_call(
        _mixin_kernel,
        grid=(n_rows // tm, MIX_MAIN // tn),
        in_specs=[
            pl.BlockSpec((tm, D_MODEL), lambda i, j: (i, 0)),
            pl.BlockSpec((None, 1, D_MODEL), lambda i, j: (layer, 0, 0)),
            pl.BlockSpec((None, D_MODEL, tn), lambda i, j: (layer, 0, j)),
            pl.BlockSpec((None, D_MODEL, LANES), lambda i, j: (layer, 0, MIX_MAIN // LANES)),
        ],
        out_specs=[
            pl.BlockSpec((tm, tn), lambda i, j: (i, j)),
            pl.BlockSpec((tm, GLA_GATE_RANK), lambda i, j: (i, 0)),
        ],
        out_shape=[
            jax.ShapeDtypeStruct((n_rows, MIX_MAIN), F32),
            jax.ShapeDtypeStruct((n_rows, GLA_GATE_RANK), F32),
        ],
        scratch_shapes=[pltpu.VMEM((tm, D_MODEL), BF16)],
        compiler_params=pltpu.CompilerParams(
            dimension_semantics=("parallel", "arbitrary"),
            vmem_limit_bytes=VMEM_LIMIT),
        name="mix_in",
    )(h, g, w, w)


def _gelu(x):
    return 0.5 * x * (1.0 + lax.erf(x * (2.0 ** -0.5)))


def _sg_kernel(u_ref, v_ref, gain_ref, w_ref, b_ref, o_ref):
    row = lax.broadcasted_iota(jnp.int32, (SG_CHUNK, SG_CHUNK), 0)
    col = lax.broadcasted_iota(jnp.int32, (SG_CHUNK, SG_CHUNK), 1)
    causal = row >= col
    for hd in range(SG_HEADS):
        cols = slice(hd * SG_HEAD_DIM, (hd + 1) * SG_HEAD_DIM)
        w = jnp.where(causal, w_ref[hd], 0.0).astype(BF16)
        bias = b_ref[hd]
        gain = gain_ref[:, cols]
        for c in range(SG_TQ // SG_CHUNK):
            rows = slice(c * SG_CHUNK, (c + 1) * SG_CHUNK)
            u = _gelu(u_ref[rows, cols])
            v = _gelu(v_ref[rows, cols])
            vc = v - jnp.mean(v, axis=-1, keepdims=True)
            var = jnp.mean(vc * vc, axis=-1, keepdims=True)
            vn = vc * lax.rsqrt(var + EPS) * gain
            mixed = _dot(w, vn.astype(BF16)) + bias
            o_ref[rows, cols] = (u * mixed).astype(o_ref.dtype)


def _spatial_gating(proj, v_gain, w_s, b_s, layer):
    n_rows = proj.shape[0]
    tq = SG_TQ
    return pl.pallas_call(
        _sg_kernel,
        grid=(n_rows // tq,),
        in_specs=[
            pl.BlockSpec((tq, SG_WIDTH), lambda i: (i, 0)),
            pl.BlockSpec((tq, SG_WIDTH), lambda i: (i, 1)),
            pl.BlockSpec((None, 1, SG_WIDTH), lambda i: (layer, 0, 0)),
            pl.BlockSpec((None, SG_HEADS, SG_CHUNK, SG_CHUNK), lambda i: (layer, 0, 0, 0)),
            pl.BlockSpec((None, SG_HEADS, SG_CHUNK, 1), lambda i: (layer, 0, 0, 0)),
        ],
        out_specs=pl.BlockSpec((tq, SG_WIDTH), lambda i: (i, 0)),
        out_shape=jax.ShapeDtypeStruct((n_rows, SG_WIDTH), BF16),
        compiler_params=pltpu.CompilerParams(
            dimension_semantics=("parallel",),
            vmem_limit_bytes=VMEM_LIMIT),
        name="spatial_gating",
    )(proj, proj, v_gain, w_s, b_s)


def _split3(x):
    x1 = x.astype(BF16)
    r1 = x - x1.astype(F32)
    x2 = r1.astype(BF16)
    x3 = (r1 - x2.astype(F32)).astype(BF16)
    return x1, x2, x3


def _group_row(x, group, idx):
    n, d = x.shape
    x3 = x.reshape(n // group, group, d)
    return jnp.broadcast_to(x3[:, idx:idx + 1, :], x3.shape).reshape(n, d)


def _gla_kernel(q_ref, k_ref, v_ref, r_ref, z_ref, wg_ref, bg_ref, og_ref, o_ref, st_ref):
    C = GLA_CHUNK

    @pl.when(pl.program_id(1) == 0)
    def _():
        st_ref[...] = jnp.zeros_like(st_ref)

    row = lax.broadcasted_iota(jnp.int32, (C, C), 0)
    col = lax.broadcasted_iota(jnp.int32, (C, C), 1)
    causal = row >= col
    tril = jnp.where(causal, 1.0, 0.0).astype(BF16)
    levels = []
    s = C // 2
    while s >= GLA_SUB:
        levels.append((s, ((row // s) % 2 == 1) & ((col // s) == (row // s) - 1)))
        s //= 2
    diag_off = col - (row // GLA_SUB) * GLA_SUB
    w_gate = wg_ref[...].astype(BF16)
    b_gate = bg_ref[...]
    g_scale = math.log2(math.e) / GLA_GATE_TAU

    def chunk(c, carry):
        rows = pl.ds(pl.multiple_of(c * C, C), C)
        pre = _dot(z_ref[0, rows, :].astype(BF16), w_gate) + b_gate
        g = (jnp.minimum(pre, 0.0) - jnp.log1p(jnp.exp(-jnp.abs(pre)))) * g_scale
        g1, g2, g3 = _split3(g)
        b_all = _dot(tril, g1) + _dot(tril, g2) + _dot(tril, g3)

        for hd in range(GLA_HEADS):
            kc = slice(hd * GLA_DK, (hd + 1) * GLA_DK)
            vc = slice(hd * GLA_DV, (hd + 1) * GLA_DV)
            b = b_all[:, kc]
            q = q_ref[0, rows, kc] * (GLA_DK ** -0.5)
            k = k_ref[0, rows, kc]
            v16 = v_ref[0, rows, vc].astype(BF16)
            b_last = b[C - 1:C, :]

            st = st_ref[hd]
            o = _dot_nt((q * jnp.exp2(b)).astype(BF16), st.astype(BF16))

            sc = jnp.zeros((C, C), F32)
            for s, mask in levels:
                d = b - _group_row(b, 2 * s, s - 1)
                qs = q * jnp.exp2(jnp.minimum(d, 0.0))
                ks = k * jnp.exp2(jnp.minimum(-d, 0.0))
                sc = jnp.where(mask, _dot_nt(qs.astype(BF16), ks.astype(BF16)), sc)

            for j in range(GLA_SUB):
                e = jnp.exp2(b - _group_row(b, GLA_SUB, j))
                sj = jnp.sum(q * _group_row(k, GLA_SUB, j) * e, axis=-1, keepdims=True)
                sc = jnp.where(diag_off == j, sj, sc)
            sc = jnp.where(causal, sc, 0.0)
            o = o + _dot(sc.astype(BF16), v16)

            k_dec = k * jnp.exp2(b_last - b)
            st_ref[hd] = st * jnp.exp2(b_last) + _dot_tn(v16, k_dec.astype(BF16))

            ms = jnp.mean(o * o, axis=-1, keepdims=True)
            r = r_ref[0, rows, vc]
            y = o * lax.rsqrt(ms + EPS) * og_ref[:, vc] * (r * jax.nn.sigmoid(r))
            o_ref[0, rows, vc] = y.astype(o_ref.dtype)
        return carry

    lax.fori_loop(0, GLA_TILE // C, chunk, 0)


def _gla(proj3, z3, w_gate, b_gate, o_gain, layer):
    bsz, t_len, _ = proj3.shape
    tile = GLA_TILE
    return pl.pallas_call(
        _gla_kernel,
        grid=(bsz, t_len // tile),
        in_specs=[
            pl.BlockSpec((1, tile, GLA_KW), lambda b, t: (b, t, OFF_Q // GLA_KW)),
            pl.BlockSpec((1, tile, GLA_KW), lambda b, t: (b, t, OFF_K // GLA_KW)),
            pl.BlockSpec((1, tile, GLA_VW), lambda b, t: (b, t, OFF_V // GLA_VW)),
            pl.BlockSpec((1, tile, GLA_VW), lambda b, t: (b, t, OFF_R // GLA_VW)),
            pl.BlockSpec((1, tile, GLA_GATE_RANK), lambda b, t: (b, t, 0)),
            pl.BlockSpec((None, GLA_GATE_RANK, GLA_KW), lambda b, t: (layer, 0, 0)),
            pl.BlockSpec((None, 1, GLA_KW), lambda b, t: (layer, 0, 0)),
            pl.BlockSpec((None, 1, GLA_VW), lambda b, t: (layer, 0, 0)),
        ],
        out_specs=pl.BlockSpec((1, tile, GLA_VW), lambda b, t: (b, t, 0)),
        out_shape=jax.ShapeDtypeStruct((bsz, t_len, GLA_VW), BF16),
        scratch_shapes=[pltpu.VMEM((GLA_HEADS, GLA_DV, GLA_DK), F32)],
        compiler_params=pltpu.CompilerParams(
            dimension_semantics=("parallel", "arbitrary"),
            vmem_limit_bytes=VMEM_LIMIT),
        name="gla",
    )(proj3, proj3, proj3, proj3, z3, w_gate, b_gate, o_gain)


def _mixout_kernel(h_ref, ya_ref, yb_ref, w_ref, o_ref, w16_ref):
    @pl.when(pl.program_id(0) == 0)
    def _():
        w16_ref[...] = w_ref[...].astype(BF16)

    acc = _dot(ya_ref[...], w16_ref[:SG_WIDTH, :])
    acc += _dot(yb_ref[...], w16_ref[SG_WIDTH:, :])
    o_ref[...] = h_ref[...] + acc


def _mixout(h, ya, yb, w, layer):
    n_rows = h.shape[0]
    tm = OUT_TM
    return pl.pallas_call(
        _mixout_kernel,
        grid=(n_rows // tm,),
        in_specs=[
            pl.BlockSpec((tm, D_MODEL), lambda i: (i, 0)),
            pl.BlockSpec((tm, SG_WIDTH), lambda i: (i, 0)),
            pl.BlockSpec((tm, GLA_VW), lambda i: (i, 0)),
            pl.BlockSpec((None, SG_WIDTH + GLA_VW, D_MODEL), lambda i: (layer, 0, 0),
                         pipeline_mode=pl.Buffered(1)),
        ],
        out_specs=pl.BlockSpec((tm, D_MODEL), lambda i: (i, 0)),
        out_shape=jax.ShapeDtypeStruct((n_rows, D_MODEL), F32),
        scratch_shapes=[pltpu.VMEM((SG_WIDTH + GLA_VW, D_MODEL), BF16)],
        compiler_params=pltpu.CompilerParams(
            dimension_semantics=("arbitrary",),
            vmem_limit_bytes=VMEM_LIMIT),
        name="mix_out",
    )(h, ya, yb, w)


def _ple_kernel(h_ref, g_ref, wg_ref, p_ref, wp_ref, fg_ref, o_ref, wg16_ref, wp16_ref, *, final_norm):
    @pl.when(pl.program_id(0) == 0)
    def _():
        wg16_ref[...] = wg_ref[...].astype(BF16)
        wp16_ref[...] = wp_ref[...].astype(BF16)

    h = h_ref[...]
    n = _rms_norm_rows(h, g_ref[...]).astype(BF16)
    gate = jax.nn.sigmoid(_dot(n, wg16_ref[...]))
    emb = _dot(p_ref[...].astype(BF16), wp16_ref[...])
    out = h + gate * emb
    if final_norm:
        out = _rms_norm_rows(out, fg_ref[...])
    o_ref[...] = out


def _ple(h, g, w_gate, p, w_proj, final_g, layer, final_norm):
    n_rows = h.shape[0]
    tm = PLE_TM
    const = pl.Buffered(1)
    return pl.pallas_call(
        functools.partial(_ple_kernel, final_norm=final_norm),
        grid=(n_rows // tm,),
        in_specs=[
            pl.BlockSpec((tm, D_MODEL), lambda i: (i, 0)),
            pl.BlockSpec((None, 1, D_MODEL), lambda i: (layer, 0, 0)),
            pl.BlockSpec((None, D_MODEL, D_MODEL), lambda i: (layer, 0, 0), pipeline_mode=const),
            pl.BlockSpec((None, tm, D_PLE), lambda i: (layer, i, 0)),
            pl.BlockSpec((None, D_PLE, D_MODEL), lambda i: (layer, 0, 0), pipeline_mode=const),
            pl.BlockSpec((1, D_MODEL), lambda i: (0, 0)),
        ],
        out_specs=pl.BlockSpec((tm, D_MODEL), lambda i: (i, 0)),
        out_shape=jax.ShapeDtypeStruct((n_rows, D_MODEL), F32),
        scratch_shapes=[pltpu.VMEM((D_MODEL, D_MODEL), BF16), pltpu.VMEM((D_PLE, D_MODEL), BF16)],
        compiler_params=pltpu.CompilerParams(
            dimension_semantics=("arbitrary",),
            vmem_limit_bytes=VMEM_LIMIT),
        name="ple",
    )(h, g, w_gate, p, w_proj, final_g)


def kernel(x, p, ffn1_norm, w_ffn1_in, w_ffn1_out, mix_norm, w_mix_in, sg_v_gain, sg_w, sg_b,
           gla_w_gate, gla_b_gate, gla_o_gain, w_mix_out, ffn2_norm, w_ffn2_in, w_ffn2_out,
           ple_norm, w_ple_gate, w_ple_proj, final_norm):
    bsz, t_len, d = x.shape
    depth = p.shape[0]
    n_rows = bsz * t_len
    h = x.reshape(n_rows, d)

    def row(v):
        return v.reshape(depth, 1, v.shape[-1])

    ffn1_g, mix_g, ffn2_g, ple_g = row(ffn1_norm), row(mix_norm), row(ffn2_norm), row(ple_norm)
    p2 = p.reshape(depth, n_rows, D_PLE)
    sg_gain, sg_bias = row(sg_v_gain), sg_b.reshape(depth, SG_HEADS, SG_CHUNK, 1)
    gla_bg, gla_og = row(gla_b_gate), row(gla_o_gain)
    final_g = final_norm.reshape(1, d)

    for i in range(depth):
        h = _ffn(h, ffn1_g, w_ffn1_in, w_ffn1_out, i)
        proj, z = _mixin(h, mix_g, w_mix_in, i)
        y_a = _spatial_gating(proj, sg_gain, sg_w, sg_bias, i)
        y_b = _gla(proj.reshape(bsz, t_len, MIX_MAIN), z.reshape(bsz, t_len, GLA_GATE_RANK),
                   gla_w_gate, gla_bg, gla_og, i)
        h = _mixout(h, y_a, y_b.reshape(n_rows, GLA_VW), w_mix_out, i)
        h = _ffn(h, ffn2_g, w_ffn2_in, w_ffn2_out, i)
        h = _ple(h, ple_g, w_ple_gate, p2, w_ple_proj, final_g, i, final_norm=(i == depth - 1))
    return h.reshape(bsz, t_len, d)
```

```python
import functools
import math

import jax
import jax.numpy as jnp
from jax import lax
from jax.experimental import pallas as pl
from jax.experimental.pallas import tpu as pltpu

F32 = jnp.float32
BF16 = jnp.bfloat16

EPS = 1e-6
D_MODEL = 2048
D_FF = 5632
D_PLE = 256

SG_HEADS = 8
SG_HEAD_DIM = 128
SG_WIDTH = SG_HEADS * SG_HEAD_DIM
SG_CHUNK = 128

GLA_HEADS = 4
GLA_DK = 128
GLA_DV = 256
GLA_KW = GLA_HEADS * GLA_DK
GLA_VW = GLA_HEADS * GLA_DV
GLA_GATE_RANK = 16
GLA_GATE_TAU = 16.0
GLA_CHUNK = 64
GLA_SUB = 8

MIX_MAIN = 2 * SG_WIDTH + 2 * GLA_KW + 2 * GLA_VW
OFF_Q = 2 * SG_WIDTH
OFF_K = OFF_Q + GLA_KW
OFF_V = OFF_K + GLA_KW
OFF_R = OFF_V + GLA_VW

VMEM_LIMIT = 56 * 1024 * 1024
LANES = 128

FFN_TM = 1024
FFN_TF = 256
MIX_TM = 1024
MIX_TN = 1024
GLA_TILE = 512
OUT_TM = 256
PLE_TM = 256


def _rms_norm_rows(h, g):
    ms = jnp.mean(h * h, axis=-1, keepdims=True)
    return h * lax.rsqrt(ms + EPS) * g


def _dot(a, b):
    return jnp.dot(a, b, preferred_element_type=F32)


def _dot_nt(a, b):
    return lax.dot_general(a, b, (((1,), (1,)), ((), ())), preferred_element_type=F32)


def _dot_tn(a, b):
    return lax.dot_general(a, b, (((0,), (0,)), ((), ())), preferred_element_type=F32)


def _ffn_kernel(h_ref, g_ref, wg_ref, wu_ref, wo_ref, o_ref, n_ref):
    @pl.when(pl.program_id(1) == 0)
    def _():
        h = h_ref[...]
        n_ref[...] = _rms_norm_rows(h, g_ref[...]).astype(BF16)
        o_ref[...] = h

    n = n_ref[...]
    gate = _dot(n, wg_ref[...].astype(BF16))
    up = _dot(n, wu_ref[...].astype(BF16))
    act = (0.5 * gate) * jax.nn.sigmoid(gate) * up
    o_ref[...] += _dot(act.astype(BF16), wo_ref[...].astype(BF16))


def _ffn(h, g, w_in, w_out, layer):
    n_rows = h.shape[0]
    tm, tf = FFN_TM, FFN_TF
    nf = D_FF // tf
    return pl.pallas_call(
        _ffn_kernel,
        grid=(n_rows // tm, nf),
        in_specs=[
            pl.BlockSpec((tm, D_MODEL), lambda i, j: (i, 0)),
            pl.BlockSpec((None, 1, D_MODEL), lambda i, j: (layer, 0, 0)),
            pl.BlockSpec((None, D_MODEL, tf), lambda i, j: (layer, 0, j)),
            pl.BlockSpec((None, D_MODEL, tf), lambda i, j: (layer, 0, j + nf)),
            pl.BlockSpec((None, tf, D_MODEL), lambda i, j: (layer, j, 0)),
        ],
        out_specs=pl.BlockSpec((tm, D_MODEL), lambda i, j: (i, 0)),
        out_shape=jax.ShapeDtypeStruct((n_rows, D_MODEL), F32),
        scratch_shapes=[pltpu.VMEM((tm, D_MODEL), BF16)],
        compiler_params=pltpu.CompilerParams(
            dimension_semantics=("parallel", "arbitrary"),
            vmem_limit_bytes=VMEM_LIMIT),
        name="ffn",
    )(h, g, w_in, w_in, w_out)


def _mixin_kernel(h_ref, g_ref, w_ref, wz_ref, o_ref, z_ref, n_ref):
    @pl.when(pl.program_id(1) == 0)
    def _():
        n = _rms_norm_rows(h_ref[...], g_ref[...]).astype(BF16)
        n_ref[...] = n
        z_ref[...] = _dot(n, wz_ref[:, :GLA_GATE_RANK])

    o_ref[...] = _dot(n_ref[...], w_ref[...])


def _mixin(h, g, w, layer):
    n_rows = h.shape[0]
    tm, tn = MIX_TM, MIX_TN
    return pl.pallas_call(
        _mixin_kernel,
        grid=(n_rows // tm, MIX_MAIN // tn),
        in_specs=[
            pl.BlockSpec((tm, D_MODEL), lambda i, j: (i, 0)),
            pl.BlockSpec((None, 1, D_MODEL), lambda i, j: (layer, 0, 0)),
            pl.BlockSpec((None, D_MODEL, tn), lambda i, j: (layer, 0, j)),
            pl.BlockSpec((None, D_MODEL, LANES), lambda i, j: (layer, 0, MIX_MAIN // LANES)),
        ],
        out_specs=[
            pl.BlockSpec((tm, tn), lambda i, j: (i, j)),
            pl.BlockSpec((tm, GLA_GATE_RANK), lambda i, j: (i, 0)),
        ],
        out_shape=[
            jax.ShapeDtypeStruct((n_rows, MIX_MAIN), F32),
            jax.ShapeDtypeStruct((n_rows, GLA_GATE_RANK), F32),
        ],
        scratch_shapes=[pltpu.VMEM((tm, D_MODEL), BF16)],
        compiler_params=pltpu.CompilerParams(
            dimension_semantics=("parallel", "arbitrary"),
            vmem_limit_bytes=VMEM_LIMIT),
        name="mix_in",
    )(h, g, w, w)


def _gelu(x):
    return 0.5 * x * (1.0 + lax.erf(x * (2.0 ** -0.5)))


def _sg_mixout_kernel(h_ref, u_ref, v_ref, gain_ref, ws_ref, bs_ref, yb_ref, w_ref, o_ref, w16_ref, ya_ref):
    @pl.when(pl.program_id(0) == 0)
    def _():
        w16_ref[...] = w_ref[...].astype(BF16)

    row = lax.broadcasted_iota(jnp.int32, (SG_CHUNK, SG_CHUNK), 0)
    col = lax.broadcasted_iota(jnp.int32, (SG_CHUNK, SG_CHUNK), 1)
    causal = row >= col
    acc = _dot(yb_ref[...], w16_ref[SG_WIDTH:, :])
    heads_per_dot = 2
    for h0 in range(0, SG_HEADS, heads_per_dot):
        for hd in range(h0, h0 + heads_per_dot):
            cols = slice(hd * SG_HEAD_DIM, (hd + 1) * SG_HEAD_DIM)
            w = jnp.where(causal, ws_ref[hd], 0.0).astype(BF16)
            bias = bs_ref[hd]
            gain = gain_ref[:, cols]
            for c in range(OUT_TM // SG_CHUNK):
                rows = slice(c * SG_CHUNK, (c + 1) * SG_CHUNK)
                u = _gelu(u_ref[rows, cols])
                v = _gelu(v_ref[rows, cols])
                vc = v - jnp.mean(v, axis=-1, keepdims=True)
                var = jnp.mean(vc * vc, axis=-1, keepdims=True)
                vn = vc * lax.rsqrt(var + EPS) * gain
                mixed = _dot(w, vn.astype(BF16)) + bias
                ya_ref[rows, cols] = (u * mixed).astype(BF16)
        kc = slice(h0 * SG_HEAD_DIM, (h0 + heads_per_dot) * SG_HEAD_DIM)
        acc += _dot(ya_ref[:, kc], w16_ref[kc, :])
    o_ref[...] = h_ref[...] + acc


def _sg_mixout(h, proj, v_gain, w_s, b_s, yb, w, layer):
    n_rows = h.shape[0]
    tm = OUT_TM
    return pl.pallas_call(
        _sg_mixout_kernel,
        grid=(n_rows // tm,),
        in_specs=[
            pl.BlockSpec((tm, D_MODEL), lambda i: (i, 0)),
            pl.BlockSpec((tm, SG_WIDTH), lambda i: (i, 0)),
            pl.BlockSpec((tm, SG_WIDTH), lambda i: (i, 1)),
            pl.BlockSpec((None, 1, SG_WIDTH), lambda i: (layer, 0, 0)),
            pl.BlockSpec((None, SG_HEADS, SG_CHUNK, SG_CHUNK), lambda i: (layer, 0, 0, 0)),
            pl.BlockSpec((None, SG_HEADS, SG_CHUNK, 1), lambda i: (layer, 0, 0, 0)),
            pl.BlockSpec((tm, GLA_VW), lambda i: (i, 0)),
            pl.BlockSpec((None, SG_WIDTH + GLA_VW, D_MODEL), lambda i: (layer, 0, 0),
                         pipeline_mode=pl.Buffered(1)),
        ],
        out_specs=pl.BlockSpec((tm, D_MODEL), lambda i: (i, 0)),
        out_shape=jax.ShapeDtypeStruct((n_rows, D_MODEL), F32),
        scratch_shapes=[pltpu.VMEM((SG_WIDTH + GLA_VW, D_MODEL), BF16),
                        pltpu.VMEM((tm, SG_WIDTH), BF16)],
        compiler_params=pltpu.CompilerParams(
            dimension_semantics=("arbitrary",),
            vmem_limit_bytes=VMEM_LIMIT),
        name="sg_mix_out",
    )(h, proj, proj, v_gain, w_s, b_s, yb, w)


def _split3(x):
    x1 = x.astype(BF16)
    r1 = x - x1.astype(F32)
    x2 = r1.astype(BF16)
    x3 = (r1 - x2.astype(F32)).astype(BF16)
    return x1, x2, x3


def _group_row(x, group, idx):
    n, d = x.shape
    x3 = x.reshape(n // group, group, d)
    return jnp.broadcast_to(x3[:, idx:idx + 1, :], x3.shape).reshape(n, d)


def _gla_kernel(q_ref, k_ref, v_ref, r_ref, z_ref, wg_ref, bg_ref, og_ref, o_ref, st_ref):
    C = GLA_CHUNK

    @pl.when(pl.program_id(1) == 0)
    def _():
        st_ref[...] = jnp.zeros_like(st_ref)

    row = lax.broadcasted_iota(jnp.int32, (C, C), 0)
    col = lax.broadcasted_iota(jnp.int32, (C, C), 1)
    causal = row >= col
    tril = jnp.where(causal, 1.0, 0.0).astype(BF16)
    levels = []
    s = C // 2
    while s >= GLA_SUB:
        levels.append((s, ((row // s) % 2 == 1) & ((col // s) == (row // s) - 1)))
        s //= 2
    diag_off = col - (row // GLA_SUB) * GLA_SUB
    w_gate = wg_ref[...].astype(BF16)
    b_gate = bg_ref[...]
    g_scale = math.log2(math.e) / GLA_GATE_TAU

    def chunk(c, carry):
        rows = pl.ds(pl.multiple_of(c * C, C), C)
        pre = _dot(z_ref[0, rows, :].astype(BF16), w_gate) + b_gate
        g = (jnp.minimum(pre, 0.0) - jnp.log1p(jnp.exp(-jnp.abs(pre)))) * g_scale
        g1, g2, g3 = _split3(g)
        b_all = _dot(tril, g1) + _dot(tril, g2) + _dot(tril, g3)

        for hd in range(GLA_HEADS):
            kc = slice(hd * GLA_DK, (hd + 1) * GLA_DK)
            vc = slice(hd * GLA_DV, (hd + 1) * GLA_DV)
            b = b_all[:, kc]
            q = q_ref[0, rows, kc] * (GLA_DK ** -0.5)
            k = k_ref[0, rows, kc]
            v16 = v_ref[0, rows, vc].astype(BF16)
            b_last = b[C - 1:C, :]

            st = st_ref[hd]
            o = _dot_nt((q * jnp.exp2(b)).astype(BF16), st.astype(BF16))

            sc = jnp.zeros((C, C), F32)
            for s, mask in levels:
                d = b - _group_row(b, 2 * s, s - 1)
                qs = q * jnp.exp2(jnp.minimum(d, 0.0))
                ks = k * jnp.exp2(jnp.minimum(-d, 0.0))
                sc = jnp.where(mask, _dot_nt(qs.astype(BF16), ks.astype(BF16)), sc)

            for j in range(GLA_SUB):
                e = jnp.exp2(b - _group_row(b, GLA_SUB, j))
                sj = jnp.sum(q * _group_row(k, GLA_SUB, j) * e, axis=-1, keepdims=True)
                sc = jnp.where(diag_off == j, sj, sc)
            sc = jnp.where(causal, sc, 0.0)
            o = o + _dot(sc.astype(BF16), v16)

            k_dec = k * jnp.exp2(b_last - b)
            st_ref[hd] = st * jnp.exp2(b_last) + _dot_tn(v16, k_dec.astype(BF16))

            ms = jnp.mean(o * o, axis=-1, keepdims=True)
            r = r_ref[0, rows, vc]
            y = o * lax.rsqrt(ms + EPS) * og_ref[:, vc] * (r * jax.nn.sigmoid(r))
            o_ref[0, rows, vc] = y.astype(o_ref.dtype)
        return carry

    lax.fori_loop(0, GLA_TILE // C, chunk, 0)


def _gla(proj3, z3, w_gate, b_gate, o_gain, layer):
    bsz, t_len, _ = proj3.shape
    tile = GLA_TILE
    return pl.pallas_call(
        _gla_kernel,
        grid=(bsz, t_len // tile),
        in_specs=[
            pl.BlockSpec((1, tile, GLA_KW), lambda b, t: (b, t, OFF_Q // GLA_KW)),
            pl.BlockSpec((1, tile, GLA_KW), lambda b, t: (b, t, OFF_K // GLA_KW)),
            pl.BlockSpec((1, tile, GLA_VW), lambda b, t: (b, t, OFF_V // GLA_VW)),
            pl.BlockSpec((1, tile, GLA_VW), lambda b, t: (b, t, OFF_R // GLA_VW)),
            pl.BlockSpec((1, tile, GLA_GATE_RANK), lambda b, t: (b, t, 0)),
            pl.BlockSpec((None, GLA_GATE_RANK, GLA_KW), lambda b, t: (layer, 0, 0)),
            pl.BlockSpec((None, 1, GLA_KW), lambda b, t: (layer, 0, 0)),
            pl.BlockSpec((None, 1, GLA_VW), lambda b, t: (layer, 0, 0)),
        ],
        out_specs=pl.BlockSpec((1, tile, GLA_VW), lambda b, t: (b, t, 0)),
        out_shape=jax.ShapeDtypeStruct((bsz, t_len, GLA_VW), BF16),
        scratch_shapes=[pltpu.VMEM((GLA_HEADS, GLA_DV, GLA_DK), F32)],
        compiler_params=pltpu.CompilerParams(
            dimension_semantics=("parallel", "arbitrary"),
            vmem_limit_bytes=VMEM_LIMIT),
        name="gla",
    )(proj3, proj3, proj3, proj3, z3, w_gate, b_gate, o_gain)


def _ple_kernel(h_ref, g_ref, wg_ref, p_ref, wp_ref, fg_ref, o_ref, wg16_ref, wp16_ref, *, final_norm):
    @pl.when(pl.program_id(0) == 0)
    def _():
        wg16_ref[...] = wg_ref[...].astype(BF16)
        wp16_ref[...] = wp_ref[...].astype(BF16)

    h = h_ref[...]
    n = _rms_norm_rows(h, g_ref[...]).astype(BF16)
    gate = jax.nn.sigmoid(_dot(n, wg16_ref[...]))
    emb = _dot(p_ref[...].astype(BF16), wp16_ref[...])
    out = h + gate * emb
    if final_norm:
        out = _rms_norm_rows(out, fg_ref[...])
    o_ref[...] = out


def _ple(h, g, w_gate, p, w_proj, final_g, layer, final_norm):
    n_rows = h.shape[0]
    tm = PLE_TM
    const = pl.Buffered(1)
    return pl.pallas_call(
        functools.partial(_ple_kernel, final_norm=final_norm),
        grid=(n_rows // tm,),
        in_specs=[
            pl.BlockSpec((tm, D_MODEL), lambda i: (i, 0)),
            pl.BlockSpec((None, 1, D_MODEL), lambda i: (layer, 0, 0)),
            pl.BlockSpec((None, D_MODEL, D_MODEL), lambda i: (layer, 0, 0), pipeline_mode=const),
            pl.BlockSpec((None, tm, D_PLE), lambda i: (layer, i, 0)),
            pl.BlockSpec((None, D_PLE, D_MODEL), lambda i: (layer, 0, 0), pipeline_mode=const),
            pl.BlockSpec((1, D_MODEL), lambda i: (0, 0)),
        ],
        out_specs=pl.BlockSpec((tm, D_MODEL), lambda i: (i, 0)),
        out_shape=jax.ShapeDtypeStruct((n_rows, D_MODEL), F32),
        scratch_shapes=[pltpu.VMEM((D_MODEL, D_MODEL), BF16), pltpu.VMEM((D_PLE, D_MODEL), BF16)],
        compiler_params=pltpu.CompilerParams(
            dimension_semantics=("arbitrary",),
            vmem_limit_bytes=VMEM_LIMIT),
        name="ple",
    )(h, g, w_gate, p, w_proj, final_g)


def kernel(x, p, ffn1_norm, w_ffn1_in, w_ffn1_out, mix_norm, w_mix_in, sg_v_gain, sg_w, sg_b,
           gla_w_gate, gla_b_gate, gla_o_gain, w_mix_out, ffn2_norm, w_ffn2_in, w_ffn2_out,
           ple_norm, w_ple_gate, w_ple_proj, final_norm):
    bsz, t_len, d = x.shape
    depth = p.shape[0]
    n_rows = bsz * t_len
    h = x.reshape(n_rows, d)

    def row(v):
        return v.reshape(depth, 1, v.shape[-1])

    ffn1_g, mix_g, ffn2_g, ple_g = row(ffn1_norm), row(mix_norm), row(ffn2_norm), row(ple_norm)
    w_mi = w_mix_in.astype(BF16)
    p2 = p.reshape(depth, n_rows, D_PLE)
    sg_gain, sg_bias = row(sg_v_gain), sg_b.reshape(depth, SG_HEADS, SG_CHUNK, 1)
    gla_bg, gla_og = row(gla_b_gate), row(gla_o_gain)
    final_g = final_norm.reshape(1, d)

    for i in range(depth):
        h = _ffn(h, ffn1_g, w_ffn1_in, w_ffn1_out, i)
        proj, z = _mixin(h, mix_g, w_mi, i)
        y_b = _gla(proj.reshape(bsz, t_len, MIX_MAIN), z.reshape(bsz, t_len, GLA_GATE_RANK),
                   gla_w_gate, gla_bg, gla_og, i)
        h = _sg_mixout(h, proj, sg_gain, sg_w, sg_bias, y_b.reshape(n_rows, GLA_VW), w_mix_out, i)
        h = _ffn(h, ffn2_g, w_ffn2_in, w_ffn2_out, i)
        h = _ple(h, ple_g, w_ple_gate, p2, w_ple_proj, final_g, i, final_norm=(i == depth - 1))
    return h.reshape(bsz, t_len, d)
```

```python
import functools
import math

import jax
import jax.numpy as jnp
from jax import lax
from jax.experimental import pallas as pl
from jax.experimental.pallas import tpu as pltpu

F32 = jnp.float32
BF16 = jnp.bfloat16

EPS = 1e-6
D_MODEL = 2048
D_FF = 5632
D_PLE = 256

SG_HEADS = 8
SG_HEAD_DIM = 128
SG_WIDTH = SG_HEADS * SG_HEAD_DIM
SG_CHUNK = 128

GLA_HEADS = 4
GLA_DK = 128
GLA_DV = 256
GLA_KW = GLA_HEADS * GLA_DK
GLA_VW = GLA_HEADS * GLA_DV
GLA_GATE_RANK = 16
GLA_GATE_TAU = 16.0
GLA_CHUNK = 64
GLA_SUB = 8

MIX_MAIN = 2 * SG_WIDTH + 2 * GLA_KW + 2 * GLA_VW
OFF_Q = 2 * SG_WIDTH

VMEM_LIMIT = 56 * 1024 * 1024
LANES = 128

FFN_TM = 1024
FFN_TF = 256
MIX_TILE = 256
OUT_TM = 256
PLE_TM = 256


def _rms_norm_rows(h, g):
    ms = jnp.mean(h * h, axis=-1, keepdims=True)
    return h * lax.rsqrt(ms + EPS) * g


def _dot(a, b):
    return jnp.dot(a, b, preferred_element_type=F32)


def _dot_nt(a, b):
    return lax.dot_general(a, b, (((1,), (1,)), ((), ())), preferred_element_type=F32)


def _dot_tn(a, b):
    return lax.dot_general(a, b, (((0,), (0,)), ((), ())), preferred_element_type=F32)


def _ffn_kernel(h_ref, g_ref, wg_ref, wu_ref, wo_ref, o_ref, n_ref):
    @pl.when(pl.program_id(1) == 0)
    def _():
        h = h_ref[...]
        n_ref[...] = _rms_norm_rows(h, g_ref[...]).astype(BF16)
        o_ref[...] = h

    n = n_ref[...]
    gate = _dot(n, wg_ref[...].astype(BF16))
    up = _dot(n, wu_ref[...].astype(BF16))
    act = (0.5 * gate) * jax.nn.sigmoid(gate) * up
    o_ref[...] += _dot(act.astype(BF16), wo_ref[...].astype(BF16))


def _ffn(h, g, w_in, w_out, layer):
    n_rows = h.shape[0]
    tm, tf = FFN_TM, FFN_TF
    nf = D_FF // tf
    return pl.pallas_call(
        _ffn_kernel,
        grid=(n_rows // tm, nf),
        in_specs=[
            pl.BlockSpec((tm, D_MODEL), lambda i, j: (i, 0)),
            pl.BlockSpec((None, 1, D_MODEL), lambda i, j: (layer, 0, 0)),
            pl.BlockSpec((None, D_MODEL, tf), lambda i, j: (layer, 0, j)),
            pl.BlockSpec((None, D_MODEL, tf), lambda i, j: (layer, 0, j + nf)),
            pl.BlockSpec((None, tf, D_MODEL), lambda i, j: (layer, j, 0)),
        ],
        out_specs=pl.BlockSpec((tm, D_MODEL), lambda i, j: (i, 0)),
        out_shape=jax.ShapeDtypeStruct((n_rows, D_MODEL), F32),
        scratch_shapes=[pltpu.VMEM((tm, D_MODEL), BF16)],
        compiler_params=pltpu.CompilerParams(
            dimension_semantics=("parallel", "arbitrary"),
            vmem_limit_bytes=VMEM_LIMIT),
        name="ffn",
    )(h, g, w_in, w_in, w_out)


def _gelu(x):
    return 0.5 * x * (1.0 + lax.erf(x * (2.0 ** -0.5)))


def _sg_mixout_kernel(h_ref, u_ref, v_ref, gain_ref, ws_ref, bs_ref, yb_ref, w_ref, o_ref, w16_ref, ya_ref):
    @pl.when(pl.program_id(0) == 0)
    def _():
        w16_ref[...] = w_ref[...].astype(BF16)

    row = lax.broadcasted_iota(jnp.int32, (SG_CHUNK, SG_CHUNK), 0)
    col = lax.broadcasted_iota(jnp.int32, (SG_CHUNK, SG_CHUNK), 1)
    causal = row >= col
    acc = _dot(yb_ref[...], w16_ref[SG_WIDTH:, :])
    heads_per_dot = 2
    for h0 in range(0, SG_HEADS, heads_per_dot):
        for hd in range(h0, h0 + heads_per_dot):
            cols = slice(hd * SG_HEAD_DIM, (hd + 1) * SG_HEAD_DIM)
            w = jnp.where(causal, ws_ref[hd], 0.0).astype(BF16)
            bias = bs_ref[hd]
            gain = gain_ref[:, cols]
            for c in range(OUT_TM // SG_CHUNK):
                rows = slice(c * SG_CHUNK, (c + 1) * SG_CHUNK)
                u = _gelu(u_ref[rows, cols])
                v = _gelu(v_ref[rows, cols])
                vc = v - jnp.mean(v, axis=-1, keepdims=True)
                var = jnp.mean(vc * vc, axis=-1, keepdims=True)
                vn = vc * lax.rsqrt(var + EPS) * gain
                mixed = _dot(w, vn.astype(BF16)) + bias
                ya_ref[rows, cols] = (u * mixed).astype(BF16)
        kc = slice(h0 * SG_HEAD_DIM, (h0 + heads_per_dot) * SG_HEAD_DIM)
        acc += _dot(ya_ref[:, kc], w16_ref[kc, :])
    o_ref[...] = h_ref[...] + acc


def _sg_mixout(h, proj, v_gain, w_s, b_s, yb, w, layer):
    n_rows = h.shape[0]
    tm = OUT_TM
    return pl.pallas_call(
        _sg_mixout_kernel,
        grid=(n_rows // tm,),
        in_specs=[
            pl.BlockSpec((tm, D_MODEL), lambda i: (i, 0)),
            pl.BlockSpec((tm, SG_WIDTH), lambda i: (i, 0)),
            pl.BlockSpec((tm, SG_WIDTH), lambda i: (i, 1)),
            pl.BlockSpec((None, 1, SG_WIDTH), lambda i: (layer, 0, 0)),
            pl.BlockSpec((None, SG_HEADS, SG_CHUNK, SG_CHUNK), lambda i: (layer, 0, 0, 0)),
            pl.BlockSpec((None, SG_HEADS, SG_CHUNK, 1), lambda i: (layer, 0, 0, 0)),
            pl.BlockSpec((tm, GLA_VW), lambda i: (i, 0)),
            pl.BlockSpec((None, SG_WIDTH + GLA_VW, D_MODEL), lambda i: (layer, 0, 0),
                         pipeline_mode=pl.Buffered(1)),
        ],
        out_specs=pl.BlockSpec((tm, D_MODEL), lambda i: (i, 0)),
        out_shape=jax.ShapeDtypeStruct((n_rows, D_MODEL), F32),
        scratch_shapes=[pltpu.VMEM((SG_WIDTH + GLA_VW, D_MODEL), BF16),
                        pltpu.VMEM((tm, SG_WIDTH), BF16)],
        compiler_params=pltpu.CompilerParams(
            dimension_semantics=("arbitrary",),
            vmem_limit_bytes=VMEM_LIMIT),
        name="sg_mix_out",
    )(h, proj, proj, v_gain, w_s, b_s, yb, w)


def _split3(x):
    x1 = x.astype(BF16)
    r1 = x - x1.astype(F32)
    x2 = r1.astype(BF16)
    x3 = (r1 - x2.astype(F32)).astype(BF16)
    return x1, x2, x3


def _group_row(x, group, idx):
    n, d = x.shape
    x3 = x.reshape(n // group, group, d)
    return jnp.broadcast_to(x3[:, idx:idx + 1, :], x3.shape).reshape(n, d)


def _mixfront_kernel(h_ref, g_ref, w_ref, wg_ref, bg_ref, og_ref, auv_ref, o_ref,
                     proj_ref, z_ref, proj_next_ref, z_next_ref, st_ref, *, tiles_per_seq):
    C = GLA_CHUNK
    step = pl.program_id(0)

    @pl.when(step == 0)
    def _():
        proj_ref[...] = jnp.zeros_like(proj_ref)
        z_ref[...] = jnp.zeros_like(z_ref)

    @pl.when((step == 0) | ((step - 1) % tiles_per_seq == 0))
    def _():
        st_ref[...] = jnp.zeros_like(st_ref)

    n = _rms_norm_rows(h_ref[...], g_ref[...]).astype(BF16)
    z_next_ref[...] = _dot(n, w_ref[:, MIX_MAIN:])
    n_slots = (MIX_TILE // C) * (GLA_HEADS + 1)
    slot_w = MIX_MAIN // n_slots
    slot_iter = iter(range(n_slots))

    def project_slice():
        lo = next(slot_iter) * slot_w
        res = _dot(n, w_ref[:, lo:lo + slot_w])
        if lo < OFF_Q:
            auv_ref[:, lo:lo + slot_w] = res
        else:
            proj_next_ref[:, lo - OFF_Q:lo - OFF_Q + slot_w] = res

    pj = proj_ref
    zp = z_ref

    row = lax.broadcasted_iota(jnp.int32, (C, C), 0)
    col = lax.broadcasted_iota(jnp.int32, (C, C), 1)
    causal = row >= col
    tril = jnp.where(causal, 1.0, 0.0).astype(BF16)
    levels = []
    s = C // 2
    while s >= GLA_SUB:
        levels.append((s, ((row // s) % 2 == 1) & ((col // s) == (row // s) - 1)))
        s //= 2
    diag_off = col - (row // GLA_SUB) * GLA_SUB
    w_gate = wg_ref[...].astype(BF16)
    b_gate = bg_ref[...]
    g_scale = math.log2(math.e) / GLA_GATE_TAU

    for c in range(MIX_TILE // C):
        rows = slice(c * C, (c + 1) * C)
        project_slice()
        pre = _dot(zp[rows, :].astype(BF16), w_gate) + b_gate
        g = (jnp.minimum(pre, 0.0) - jnp.log1p(jnp.exp(-jnp.abs(pre)))) * g_scale
        g1, g2, g3 = _split3(g)
        b_all = _dot(tril, g1) + _dot(tril, g2) + _dot(tril, g3)

        for hd in range(GLA_HEADS):
            kc = slice(hd * GLA_DK, (hd + 1) * GLA_DK)
            vc = slice(hd * GLA_DV, (hd + 1) * GLA_DV)
            b = b_all[:, kc]
            project_slice()
            q = pj[rows, kc] * (GLA_DK ** -0.5)
            k = pj[rows, GLA_KW + hd * GLA_DK:GLA_KW + (hd + 1) * GLA_DK]
            v16 = pj[rows, 2 * GLA_KW + hd * GLA_DV:2 * GLA_KW + (hd + 1) * GLA_DV].astype(BF16)
            b_last = b[C - 1:C, :]

            st = st_ref[hd]
            o = _dot_nt((q * jnp.exp2(b)).astype(BF16), st.astype(BF16))

            sc = jnp.zeros((C, C), F32)
            for s, mask in levels:
                d = b - _group_row(b, 2 * s, s - 1)
                qs = q * jnp.exp2(jnp.minimum(d, 0.0))
                ks = k * jnp.exp2(jnp.minimum(-d, 0.0))
                sc = jnp.where(mask, _dot_nt(qs.astype(BF16), ks.astype(BF16)), sc)

            for j in range(GLA_SUB):
                e = jnp.exp2(b - _group_row(b, GLA_SUB, j))
                sj = jnp.sum(q * _group_row(k, GLA_SUB, j) * e, axis=-1, keepdims=True)
                sc = jnp.where(diag_off == j, sj, sc)
            sc = jnp.where(causal, sc, 0.0)
            o = o + _dot(sc.astype(BF16), v16)

            k_dec = k * jnp.exp2(b_last - b)
            st_ref[hd] = st * jnp.exp2(b_last) + _dot_tn(v16, k_dec.astype(BF16))

            ms = jnp.mean(o * o, axis=-1, keepdims=True)
            r = pj[rows, 2 * GLA_KW + GLA_VW + hd * GLA_DV:2 * GLA_KW + GLA_VW + (hd + 1) * GLA_DV]
            y = o * lax.rsqrt(ms + EPS) * og_ref[:, vc] * (r * jax.nn.sigmoid(r))
            o_ref[rows, vc] = y.astype(o_ref.dtype)

    proj_ref[...] = proj_next_ref[...]
    z_ref[...] = z_next_ref[...]


def _mixfront(h, g, w, w_gate, b_gate, o_gain, layer, t_len):
    n_rows = h.shape[0]
    tile = MIX_TILE
    n_tiles = n_rows // tile
    gla_cols = MIX_MAIN - OFF_Q
    return pl.pallas_call(
        functools.partial(_mixfront_kernel, tiles_per_seq=t_len // tile),
        grid=(n_tiles + 1,),
        in_specs=[
            pl.BlockSpec((tile, D_MODEL), lambda s: (jnp.minimum(s, n_tiles - 1), 0)),
            pl.BlockSpec((None, 1, D_MODEL), lambda s: (layer, 0, 0)),
            pl.BlockSpec((None, D_MODEL, MIX_MAIN + GLA_GATE_RANK), lambda s: (layer, 0, 0),
                         pipeline_mode=pl.Buffered(1)),
            pl.BlockSpec((None, GLA_GATE_RANK, GLA_KW), lambda s: (layer, 0, 0)),
            pl.BlockSpec((None, 1, GLA_KW), lambda s: (layer, 0, 0)),
            pl.BlockSpec((None, 1, GLA_VW), lambda s: (layer, 0, 0)),
        ],
        out_specs=[
            pl.BlockSpec((tile, OFF_Q), lambda s: (jnp.minimum(s, n_tiles - 1), 0)),
            pl.BlockSpec((tile, GLA_VW), lambda s: (jnp.maximum(s - 1, 0), 0)),
        ],
        out_shape=[
            jax.ShapeDtypeStruct((n_rows, OFF_Q), F32),
            jax.ShapeDtypeStruct((n_rows, GLA_VW), BF16),
        ],
        scratch_shapes=[
            pltpu.VMEM((tile, gla_cols), F32),
            pltpu.VMEM((tile, GLA_GATE_RANK), F32),
            pltpu.VMEM((tile, gla_cols), F32),
            pltpu.VMEM((tile, GLA_GATE_RANK), F32),
            pltpu.VMEM((GLA_HEADS, GLA_DV, GLA_DK), F32),
        ],
        compiler_params=pltpu.CompilerParams(
            dimension_semantics=("arbitrary",),
            vmem_limit_bytes=VMEM_LIMIT),
        name="mix_front",
    )(h, g, w, w_gate, b_gate, o_gain)


def _ple_kernel(h_ref, g_ref, wg_ref, p_ref, wp_ref, fg_ref, o_ref, wg16_ref, wp16_ref, *, final_norm):
    @pl.when(pl.program_id(0) == 0)
    def _():
        wg16_ref[...] = wg_ref[...].astype(BF16)
        wp16_ref[...] = wp_ref[...].astype(BF16)

    h = h_ref[...]
    n = _rms_norm_rows(h, g_ref[...]).astype(BF16)
    gate = jax.nn.sigmoid(_dot(n, wg16_ref[...]))
    emb = _dot(p_ref[...].astype(BF16), wp16_ref[...])
    out = h + gate * emb
    if final_norm:
        out = _rms_norm_rows(out, fg_ref[...])
    o_ref[...] = out


def _ple(h, g, w_gate, p, w_proj, final_g, layer, final_norm):
    n_rows = h.shape[0]
    tm = PLE_TM
    const = pl.Buffered(1)
    return pl.pallas_call(
        functools.partial(_ple_kernel, final_norm=final_norm),
        grid=(n_rows // tm,),
        in_specs=[
            pl.BlockSpec((tm, D_MODEL), lambda i: (i, 0)),
            pl.BlockSpec((None, 1, D_MODEL), lambda i: (layer, 0, 0)),
            pl.BlockSpec((None, D_MODEL, D_MODEL), lambda i: (layer, 0, 0), pipeline_mode=const),
            pl.BlockSpec((None, tm, D_PLE), lambda i: (layer, i, 0)),
            pl.BlockSpec((None, D_PLE, D_MODEL), lambda i: (layer, 0, 0), pipeline_mode=const),
            pl.BlockSpec((1, D_MODEL), lambda i: (0, 0)),
        ],
        out_specs=pl.BlockSpec((tm, D_MODEL), lambda i: (i, 0)),
        out_shape=jax.ShapeDtypeStruct((n_rows, D_MODEL), F32),
        scratch_shapes=[pltpu.VMEM((D_MODEL, D_MODEL), BF16), pltpu.VMEM((D_PLE, D_MODEL), BF16)],
        compiler_params=pltpu.CompilerParams(
            dimension_semantics=("arbitrary",),
            vmem_limit_bytes=VMEM_LIMIT),
        name="ple",
    )(h, g, w_gate, p, w_proj, final_g)


def kernel(x, p, ffn1_norm, w_ffn1_in, w_ffn1_out, mix_norm, w_mix_in, sg_v_gain, sg_w, sg_b,
           gla_w_gate, gla_b_gate, gla_o_gain, w_mix_out, ffn2_norm, w_ffn2_in, w_ffn2_out,
           ple_norm, w_ple_gate, w_ple_proj, final_norm):
    bsz, t_len, d = x.shape
    depth = p.shape[0]
    n_rows = bsz * t_len
    h = x.reshape(n_rows, d)

    def row(v):
        return v.reshape(depth, 1, v.shape[-1])

    ffn1_g, mix_g, ffn2_g, ple_g = row(ffn1_norm), row(mix_norm), row(ffn2_norm), row(ple_norm)
    w_mi = w_mix_in.astype(BF16)
    p2 = p.reshape(depth, n_rows, D_PLE)
    sg_gain, sg_bias = row(sg_v_gain), sg_b.reshape(depth, SG_HEADS, SG_CHUNK, 1)
    gla_bg, gla_og = row(gla_b_gate), row(gla_o_gain)
    final_g = final_norm.reshape(1, d)

    for i in range(depth):
        h = _ffn(h, ffn1_g, w_ffn1_in, w_ffn1_out, i)
        a_uv, y_b = _mixfront(h, mix_g, w_mi, gla_w_gate, gla_bg, gla_og, i, t_len)
        h = _sg_mixout(h, a_uv, sg_gain, sg_w, sg_bias, y_b, w_mix_out, i)
        h = _ffn(h, ffn2_g, w_ffn2_in, w_ffn2_out, i)
        h = _ple(h, ple_g, w_ple_gate, p2, w_ple_proj, final_g, i, final_norm=(i == depth - 1))
    return h.reshape(bsz, t_len, d)
```

```python
import functools
import math

import jax
import jax.numpy as jnp
from jax import lax
from jax.experimental import pallas as pl
from jax.experimental.pallas import tpu as pltpu

F32 = jnp.float32
BF16 = jnp.bfloat16

EPS = 1e-6
D_MODEL = 2048
D_FF = 5632
D_PLE = 256

SG_HEADS = 8
SG_HEAD_DIM = 128
SG_WIDTH = SG_HEADS * SG_HEAD_DIM
SG_CHUNK = 128

GLA_HEADS = 4
GLA_DK = 128
GLA_DV = 256
GLA_KW = GLA_HEADS * GLA_DK
GLA_VW = GLA_HEADS * GLA_DV
GLA_GATE_RANK = 16
GLA_GATE_TAU = 16.0
GLA_CHUNK = 64
GLA_SUB = 8

MIX_MAIN = 2 * SG_WIDTH + 2 * GLA_KW + 2 * GLA_VW
OFF_Q = 2 * SG_WIDTH

VMEM_LIMIT = 56 * 1024 * 1024

FFN_TM = 1024
FFN_TF = 512
MIX_TILE = 256
MIX_SLICE = 256
OUT_TM = 256
PLE_TM = 256


def _rms_norm_rows(h, g):
    ms = jnp.mean(h * h, axis=-1, keepdims=True)
    return h * lax.rsqrt(ms + EPS) * g


def _dot(a, b):
    return jnp.dot(a, b, preferred_element_type=F32)


def _dot_nt(a, b):
    return lax.dot_general(a, b, (((1,), (1,)), ((), ())), preferred_element_type=F32)


def _dot_tn(a, b):
    return lax.dot_general(a, b, (((0,), (0,)), ((), ())), preferred_element_type=F32)


def _norm_kernel(x_ref, g_ref, o_ref):
    o_ref[...] = _rms_norm_rows(x_ref[...], g_ref[...]).astype(o_ref.dtype)


def _norm(x, g, layer):
    n_rows = x.shape[0]
    tm = FFN_TM
    return pl.pallas_call(
        _norm_kernel,
        grid=(n_rows // tm,),
        in_specs=[
            pl.BlockSpec((tm, D_MODEL), lambda i: (i, 0)),
            pl.BlockSpec((None, 1, D_MODEL), lambda i: (layer, 0, 0)),
        ],
        out_specs=pl.BlockSpec((tm, D_MODEL), lambda i: (i, 0)),
        out_shape=jax.ShapeDtypeStruct((n_rows, D_MODEL), BF16),
        compiler_params=pltpu.CompilerParams(
            dimension_semantics=("parallel",),
            vmem_limit_bytes=VMEM_LIMIT),
        name="norm",
    )(x, g)


def _ffn_kernel(n_ref, wg_ref, wu_ref, wo_ref, o_ref):
    @pl.when(pl.program_id(1) == 0)
    def _():
        o_ref[...] = jnp.zeros_like(o_ref)

    n = n_ref[...]
    gate = _dot(n, wg_ref[...].astype(BF16))
    up = _dot(n, wu_ref[...].astype(BF16))
    act = (0.5 * gate) * jax.nn.sigmoid(gate) * up
    o_ref[...] += _dot(act.astype(BF16), wo_ref[...].astype(BF16))


def _ffn(n, w_in, w_out, layer):
    n_rows = n.shape[0]
    tm, tf = FFN_TM, FFN_TF
    nf = D_FF // tf
    return pl.pallas_call(
        _ffn_kernel,
        grid=(n_rows // tm, nf),
        in_specs=[
            pl.BlockSpec((tm, D_MODEL), lambda i, j: (i, 0)),
            pl.BlockSpec((None, D_MODEL, tf), lambda i, j: (layer, 0, j)),
            pl.BlockSpec((None, D_MODEL, tf), lambda i, j: (layer, 0, j + nf)),
            pl.BlockSpec((None, tf, D_MODEL), lambda i, j: (layer, j, 0)),
        ],
        out_specs=pl.BlockSpec((tm, D_MODEL), lambda i, j: (i, 0)),
        out_shape=jax.ShapeDtypeStruct((n_rows, D_MODEL), F32),
        compiler_params=pltpu.CompilerParams(
            dimension_semantics=("parallel", "arbitrary"),
            vmem_limit_bytes=VMEM_LIMIT),
        name="ffn",
    )(n, w_in, w_in, w_out)


def _gelu(x):
    return 0.5 * x * (1.0 + lax.erf(x * (2.0 ** -0.5)))


def _sg_mixout_kernel(h_ref, d_ref, u_ref, v_ref, gain_ref, ws_ref, bs_ref, yb_ref, w_ref, gn_ref,
                      o_ref, n_ref, w16_ref, ya_ref):
    @pl.when(pl.program_id(0) == 0)
    def _():
        w16_ref[...] = w_ref[...].astype(BF16)

    row = lax.broadcasted_iota(jnp.int32, (SG_CHUNK, SG_CHUNK), 0)
    col = lax.broadcasted_iota(jnp.int32, (SG_CHUNK, SG_CHUNK), 1)
    causal = row >= col
    acc = _dot(yb_ref[...], w16_ref[SG_WIDTH:, :])
    heads_per_dot = 2
    for h0 in range(0, SG_HEADS, heads_per_dot):
        for hd in range(h0, h0 + heads_per_dot):
            cols = slice(hd * SG_HEAD_DIM, (hd + 1) * SG_HEAD_DIM)
            w = jnp.where(causal, ws_ref[hd], 0.0).astype(BF16)
            bias = bs_ref[hd]
            gain = gain_ref[:, cols]
            for c in range(OUT_TM // SG_CHUNK):
                rows = slice(c * SG_CHUNK, (c + 1) * SG_CHUNK)
                u = _gelu(u_ref[rows, cols])
                v = _gelu(v_ref[rows, cols])
                vc = v - jnp.mean(v, axis=-1, keepdims=True)
                var = jnp.mean(vc * vc, axis=-1, keepdims=True)
                vn = vc * lax.rsqrt(var + EPS) * gain
                mixed = _dot(w, vn.astype(BF16)) + bias
                ya_ref[rows, cols] = (u * mixed).astype(BF16)
        kc = slice(h0 * SG_HEAD_DIM, (h0 + heads_per_dot) * SG_HEAD_DIM)
        acc += _dot(ya_ref[:, kc], w16_ref[kc, :])
    out = (h_ref[...] + d_ref[...]) + acc
    o_ref[...] = out
    n_ref[...] = _rms_norm_rows(out, gn_ref[...]).astype(n_ref.dtype)


def _sg_mixout(h, d, proj, v_gain, w_s, b_s, yb, w, g_next, layer):
    n_rows = h.shape[0]
    tm = OUT_TM
    return pl.pallas_call(
        _sg_mixout_kernel,
        grid=(n_rows // tm,),
        in_specs=[
            pl.BlockSpec((tm, D_MODEL), lambda i: (i, 0)),
            pl.BlockSpec((tm, D_MODEL), lambda i: (i, 0)),
            pl.BlockSpec((tm, SG_WIDTH), lambda i: (i, 0)),
            pl.BlockSpec((tm, SG_WIDTH), lambda i: (i, 1)),
            pl.BlockSpec((None, 1, SG_WIDTH), lambda i: (layer, 0, 0)),
            pl.BlockSpec((None, SG_HEADS, SG_CHUNK, SG_CHUNK), lambda i: (layer, 0, 0, 0)),
            pl.BlockSpec((None, SG_HEADS, SG_CHUNK, 1), lambda i: (layer, 0, 0, 0)),
            pl.BlockSpec((tm, GLA_VW), lambda i: (i, 0)),
            pl.BlockSpec((None, SG_WIDTH + GLA_VW, D_MODEL), lambda i: (layer, 0, 0),
                         pipeline_mode=pl.Buffered(1)),
            pl.BlockSpec((None, 1, D_MODEL), lambda i: (layer, 0, 0)),
        ],
        out_specs=[pl.BlockSpec((tm, D_MODEL), lambda i: (i, 0)),
                   pl.BlockSpec((tm, D_MODEL), lambda i: (i, 0))],
        out_shape=[jax.ShapeDtypeStruct((n_rows, D_MODEL), F32),
                   jax.ShapeDtypeStruct((n_rows, D_MODEL), BF16)],
        scratch_shapes=[pltpu.VMEM((SG_WIDTH + GLA_VW, D_MODEL), BF16),
                        pltpu.VMEM((tm, SG_WIDTH), BF16)],
        compiler_params=pltpu.CompilerParams(
            dimension_semantics=("arbitrary",),
            vmem_limit_bytes=VMEM_LIMIT),
        name="sg_mix_out",
    )(h, d, proj, proj, v_gain, w_s, b_s, yb, w, g_next)


def _split3(x):
    x1 = x.astype(BF16)
    r1 = x - x1.astype(F32)
    x2 = r1.astype(BF16)
    x3 = (r1 - x2.astype(F32)).astype(BF16)
    return x1, x2, x3


def _group_row(x, group, idx):
    n, d = x.shape
    x3 = x.reshape(n // group, group, d)
    return jnp.broadcast_to(x3[:, idx:idx + 1, :], x3.shape).reshape(n, d)


def _mixfront_kernel(h_ref, d_ref, g_ref, w_ref, wg_ref, bg_ref, og_ref, auv_ref, o_ref,
                     proj_ref, z_ref, proj_next_ref, z_next_ref, st_ref, *, tiles_per_seq):
    C = GLA_CHUNK
    step = pl.program_id(0)

    @pl.when(step == 0)
    def _():
        proj_ref[...] = jnp.zeros_like(proj_ref)
        z_ref[...] = jnp.zeros_like(z_ref)

    @pl.when((step == 0) | ((step - 1) % tiles_per_seq == 0))
    def _():
        st_ref[...] = jnp.zeros_like(st_ref)

    n = _rms_norm_rows(h_ref[...] + d_ref[...], g_ref[...]).astype(BF16)
    z_next_ref[...] = _dot(n, w_ref[:, MIX_MAIN:])
    n_slots = (MIX_TILE // C) * (GLA_HEADS + 1)
    n_slices = MIX_MAIN // MIX_SLICE
    slot_iter = iter(range(n_slots))

    def project_slice():
        slot = next(slot_iter)
        for i in range(-(-slot * n_slices // n_slots), -(-(slot + 1) * n_slices // n_slots)):
            lo = i * MIX_SLICE
            res = _dot(n, w_ref[:, lo:lo + MIX_SLICE])
            if lo < OFF_Q:
                auv_ref[:, lo:lo + MIX_SLICE] = res
            else:
                proj_next_ref[:, lo - OFF_Q:lo - OFF_Q + MIX_SLICE] = res

    pj = proj_ref
    zp = z_ref

    row = lax.broadcasted_iota(jnp.int32, (C, C), 0)
    col = lax.broadcasted_iota(jnp.int32, (C, C), 1)
    causal = row >= col
    tril = jnp.where(causal, 1.0, 0.0).astype(BF16)
    levels = []
    s = C // 2
    while s >= GLA_SUB:
        levels.append((s, ((row // s) % 2 == 1) & ((col // s) == (row // s) - 1)))
        s //= 2
    diag_off = col - (row // GLA_SUB) * GLA_SUB
    w_gate = wg_ref[...].astype(BF16)
    b_gate = bg_ref[...]
    g_scale = math.log2(math.e) / GLA_GATE_TAU

    for c in range(MIX_TILE // C):
        rows = slice(c * C, (c + 1) * C)
        project_slice()
        pre = _dot(zp[rows, :].astype(BF16), w_gate) + b_gate
        g = (jnp.minimum(pre, 0.0) - jnp.log1p(jnp.exp(-jnp.abs(pre)))) * g_scale
        g1, g2, g3 = _split3(g)
        b_all = _dot(tril, g1) + _dot(tril, g2) + _dot(tril, g3)

        for hd in range(GLA_HEADS):
            kc = slice(hd * GLA_DK, (hd + 1) * GLA_DK)
            vc = slice(hd * GLA_DV, (hd + 1) * GLA_DV)
            b = b_all[:, kc]
            project_slice()
            q = pj[rows, kc] * (GLA_DK ** -0.5)
            k = pj[rows, GLA_KW + hd * GLA_DK:GLA_KW + (hd + 1) * GLA_DK]
            v16 = pj[rows, 2 * GLA_KW + hd * GLA_DV:2 * GLA_KW + (hd + 1) * GLA_DV].astype(BF16)
            b_last = b[C - 1:C, :]

            st = st_ref[hd]
            o = _dot_nt((q * jnp.exp2(b)).astype(BF16), st.astype(BF16))

            sc = jnp.zeros((C, C), F32)
            for s, mask in levels:
                d = b - _group_row(b, 2 * s, s - 1)
                qs = q * jnp.exp2(jnp.minimum(d, 0.0))
                ks = k * jnp.exp2(jnp.minimum(-d, 0.0))
                sc = jnp.where(mask, _dot_nt(qs.astype(BF16), ks.astype(BF16)), sc)

            for j in range(GLA_SUB):
                e = jnp.exp2(b - _group_row(b, GLA_SUB, j))
                sj = jnp.sum(q * _group_row(k, GLA_SUB, j) * e, axis=-1, keepdims=True)
                sc = jnp.where(diag_off == j, sj, sc)
            sc = jnp.where(causal, sc, 0.0)
            o = o + _dot(sc.astype(BF16), v16)

            k_dec = k * jnp.exp2(b_last - b)
            st_ref[hd] = st * jnp.exp2(b_last) + _dot_tn(v16, k_dec.astype(BF16))

            ms = jnp.mean(o * o, axis=-1, keepdims=True)
            r = pj[rows, 2 * GLA_KW + GLA_VW + hd * GLA_DV:2 * GLA_KW + GLA_VW + (hd + 1) * GLA_DV]
            y = o * lax.rsqrt(ms + EPS) * og_ref[:, vc] * (r * jax.nn.sigmoid(r))
            o_ref[rows, vc] = y.astype(o_ref.dtype)

    proj_ref[...] = proj_next_ref[...]
    z_ref[...] = z_next_ref[...]


def _mixfront(h, d, g, w, w_gate, b_gate, o_gain, layer, t_len):
    n_rows = h.shape[0]
    tile = MIX_TILE
    n_tiles = n_rows // tile
    gla_cols = MIX_MAIN - OFF_Q
    return pl.pallas_call(
        functools.partial(_mixfront_kernel, tiles_per_seq=t_len // tile),
        grid=(n_tiles + 1,),
        in_specs=[
            pl.BlockSpec((tile, D_MODEL), lambda s: (jnp.minimum(s, n_tiles - 1), 0)),
            pl.BlockSpec((tile, D_MODEL), lambda s: (jnp.minimum(s, n_tiles - 1), 0)),
            pl.BlockSpec((None, 1, D_MODEL), lambda s: (layer, 0, 0)),
            pl.BlockSpec((None, D_MODEL, MIX_MAIN + GLA_GATE_RANK), lambda s: (layer, 0, 0),
                         pipeline_mode=pl.Buffered(1)),
            pl.BlockSpec((None, GLA_GATE_RANK, GLA_KW), lambda s: (layer, 0, 0)),
            pl.BlockSpec((None, 1, GLA_KW), lambda s: (layer, 0, 0)),
            pl.BlockSpec((None, 1, GLA_VW), lambda s: (layer, 0, 0)),
        ],
        out_specs=[
            pl.BlockSpec((tile, OFF_Q), lambda s: (jnp.minimum(s, n_tiles - 1), 0)),
            pl.BlockSpec((tile, GLA_VW), lambda s: (jnp.maximum(s - 1, 0), 0)),
        ],
        out_shape=[
            jax.ShapeDtypeStruct((n_rows, OFF_Q), F32),
            jax.ShapeDtypeStruct((n_rows, GLA_VW), BF16),
        ],
        scratch_shapes=[
            pltpu.VMEM((tile, gla_cols), F32),
            pltpu.VMEM((tile, GLA_GATE_RANK), F32),
            pltpu.VMEM((tile, gla_cols), F32),
            pltpu.VMEM((tile, GLA_GATE_RANK), F32),
            pltpu.VMEM((GLA_HEADS, GLA_DV, GLA_DK), F32),
        ],
        compiler_params=pltpu.CompilerParams(
            dimension_semantics=("arbitrary",),
            vmem_limit_bytes=VMEM_LIMIT),
        name="mix_front",
    )(h, d, g, w, w_gate, b_gate, o_gain)


def _ple_kernel(h_ref, d_ref, g_ref, wg_ref, p_ref, wp_ref, gn_ref, *refs, final_norm):
    if final_norm:
        o_ref, wg16_ref, wp16_ref = refs
    else:
        o_ref, n_ref, wg16_ref, wp16_ref = refs

    @pl.when(pl.program_id(0) == 0)
    def _():
        wg16_ref[...] = wg_ref[...].astype(BF16)
        wp16_ref[...] = wp_ref[...].astype(BF16)

    h = h_ref[...] + d_ref[...]
    n = _rms_norm_rows(h, g_ref[...]).astype(BF16)
    gate = jax.nn.sigmoid(_dot(n, wg16_ref[...]))
    emb = _dot(p_ref[...].astype(BF16), wp16_ref[...])
    out = h + gate * emb
    if final_norm:
        o_ref[...] = _rms_norm_rows(out, gn_ref[...])
    else:
        o_ref[...] = out
        n_ref[...] = _rms_norm_rows(out, gn_ref[...]).astype(n_ref.dtype)


def _ple(h, d, g, w_gate, p, w_proj, g_next, layer, final_norm):
    n_rows = h.shape[0]
    tm = PLE_TM
    const = pl.Buffered(1)
    row_spec = pl.BlockSpec((tm, D_MODEL), lambda i: (i, 0))
    if final_norm:
        out_specs = row_spec
        out_shape = jax.ShapeDtypeStruct((n_rows, D_MODEL), F32)
    else:
        out_specs = [row_spec, row_spec]
        out_shape = [jax.ShapeDtypeStruct((n_rows, D_MODEL), F32),
                     jax.ShapeDtypeStruct((n_rows, D_MODEL), BF16)]
    return pl.pallas_call(
        functools.partial(_ple_kernel, final_norm=final_norm),
        grid=(n_rows // tm,),
        in_specs=[
            row_spec,
            row_spec,
            pl.BlockSpec((None, 1, D_MODEL), lambda i: (layer, 0, 0)),
            pl.BlockSpec((None, D_MODEL, D_MODEL), lambda i: (layer, 0, 0), pipeline_mode=const),
            pl.BlockSpec((None, tm, D_PLE), lambda i: (layer, i, 0)),
            pl.BlockSpec((None, D_PLE, D_MODEL), lambda i: (layer, 0, 0), pipeline_mode=const),
            pl.BlockSpec((1, D_MODEL), lambda i: (0, 0)),
        ],
        out_specs=out_specs,
        out_shape=out_shape,
        scratch_shapes=[pltpu.VMEM((D_MODEL, D_MODEL), BF16), pltpu.VMEM((D_PLE, D_MODEL), BF16)],
        compiler_params=pltpu.CompilerParams(
            dimension_semantics=("arbitrary",),
            vmem_limit_bytes=VMEM_LIMIT),
        name="ple",
    )(h, d, g, w_gate, p, w_proj, g_next)


def kernel(x, p, ffn1_norm, w_ffn1_in, w_ffn1_out, mix_norm, w_mix_in, sg_v_gain, sg_w, sg_b,
           gla_w_gate, gla_b_gate, gla_o_gain, w_mix_out, ffn2_norm, w_ffn2_in, w_ffn2_out,
           ple_norm, w_ple_gate, w_ple_proj, final_norm):
    bsz, t_len, d_model = x.shape
    depth = p.shape[0]
    n_rows = bsz * t_len
    h = x.reshape(n_rows, d_model)

    def row(v):
        return v.reshape(depth, 1, v.shape[-1])

    ffn1_g, mix_g, ffn2_g, ple_g = row(ffn1_norm), row(mix_norm), row(ffn2_norm), row(ple_norm)
    w_mi = w_mix_in.astype(BF16)
    p2 = p.reshape(depth, n_rows, D_PLE)
    sg_gain, sg_bias = row(sg_v_gain), sg_b.reshape(depth, SG_HEADS, SG_CHUNK, 1)
    gla_bg, gla_og = row(gla_b_gate), row(gla_o_gain)
    final_g = final_norm.reshape(1, d_model)

    n = _norm(h, ffn1_g, 0)
    for i in range(depth):
        last = i == depth - 1
        d = _ffn(n, w_ffn1_in, w_ffn1_out, i)
        a_uv, y_b = _mixfront(h, d, mix_g, w_mi, gla_w_gate, gla_bg, gla_og, i, t_len)
        h, n = _sg_mixout(h, d, a_uv, sg_gain, sg_w, sg_bias, y_b, w_mix_out, ffn2_g, i)
        d = _ffn(n, w_ffn2_in, w_ffn2_out, i)
        if last:
            h = _ple(h, d, ple_g, w_ple_gate, p2, w_ple_proj, final_g, i, final_norm=True)
        else:
            h, n = _ple(h, d, ple_g, w_ple_gate, p2, w_ple_proj, ffn1_norm[i + 1].reshape(1, d_model), i,
                        final_norm=False)
    return h.reshape(bsz, t_len, d_model)
```

```python
import functools
import math

import jax
import jax.numpy as jnp
from jax import lax
from jax.experimental import pallas as pl
from jax.experimental.pallas import tpu as pltpu

F32 = jnp.float32
BF16 = jnp.bfloat16

EPS = 1e-6
D_MODEL = 2048
D_FF = 5632
D_PLE = 256

SG_HEADS = 8
SG_HEAD_DIM = 128
SG_WIDTH = SG_HEADS * SG_HEAD_DIM
SG_CHUNK = 128

GLA_HEADS = 4
GLA_DK = 128
GLA_DV = 256
GLA_KW = GLA_HEADS * GLA_DK
GLA_VW = GLA_HEADS * GLA_DV
GLA_GATE_RANK = 16
GLA_GATE_TAU = 16.0
GLA_CHUNK = 64
GLA_SUB = 8

MIX_MAIN = 2 * SG_WIDTH + 2 * GLA_KW + 2 * GLA_VW
OFF_Q = 2 * SG_WIDTH

VMEM_LIMIT = 56 * 1024 * 1024

FFN_TM = 1024
FFN_TF = 512
MIX_TILE = 256
MIX_SLICE = 256
OUT_TM = 256
PLE_TM = 256


def _rms_norm_rows(h, g):
    ms = jnp.mean(h * h, axis=-1, keepdims=True)
    return h * lax.rsqrt(ms + EPS) * g


def _dot(a, b):
    return jnp.dot(a, b, preferred_element_type=F32)


def _dot_nt(a, b):
    return lax.dot_general(a, b, (((1,), (1,)), ((), ())), preferred_element_type=F32)


def _dot_tn(a, b):
    return lax.dot_general(a, b, (((0,), (0,)), ((), ())), preferred_element_type=F32)


def _norm_kernel(x_ref, g_ref, o_ref):
    o_ref[...] = _rms_norm_rows(x_ref[...], g_ref[...]).astype(o_ref.dtype)


def _norm(x, g, layer):
    n_rows = x.shape[0]
    tm = FFN_TM
    return pl.pallas_call(
        _norm_kernel,
        grid=(n_rows // tm,),
        in_specs=[
            pl.BlockSpec((tm, D_MODEL), lambda i: (i, 0)),
            pl.BlockSpec((None, 1, D_MODEL), lambda i: (layer, 0, 0)),
        ],
        out_specs=pl.BlockSpec((tm, D_MODEL), lambda i: (i, 0)),
        out_shape=jax.ShapeDtypeStruct((n_rows, D_MODEL), BF16),
        compiler_params=pltpu.CompilerParams(
            dimension_semantics=("parallel",),
            vmem_limit_bytes=VMEM_LIMIT),
        name="norm",
    )(x, g)


def _ffn_kernel(n_ref, wg_ref, wu_ref, wo_ref, o_ref):
    @pl.when(pl.program_id(1) == 0)
    def _():
        o_ref[...] = jnp.zeros_like(o_ref)

    n = n_ref[...]
    gate = _dot(n, wg_ref[...].astype(BF16))
    up = _dot(n, wu_ref[...].astype(BF16))
    act = (0.5 * gate) * jax.nn.sigmoid(gate) * up
    o_ref[...] += _dot(act.astype(BF16), wo_ref[...].astype(BF16))


def _ffn(n, w_in, w_out, layer):
    n_rows = n.shape[0]
    tm, tf = FFN_TM, FFN_TF
    nf = D_FF // tf
    return pl.pallas_call(
        _ffn_kernel,
        grid=(n_rows // tm, nf),
        in_specs=[
            pl.BlockSpec((tm, D_MODEL), lambda i, j: (i, 0)),
            pl.BlockSpec((None, D_MODEL, tf), lambda i, j: (layer, 0, j)),
            pl.BlockSpec((None, D_MODEL, tf), lambda i, j: (layer, 0, j + nf)),
            pl.BlockSpec((None, tf, D_MODEL), lambda i, j: (layer, j, 0)),
        ],
        out_specs=pl.BlockSpec((tm, D_MODEL), lambda i, j: (i, 0)),
        out_shape=jax.ShapeDtypeStruct((n_rows, D_MODEL), F32),
        compiler_params=pltpu.CompilerParams(
            dimension_semantics=("parallel", "arbitrary"),
            vmem_limit_bytes=VMEM_LIMIT),
        name="ffn",
    )(n, w_in, w_in, w_out)


def _gelu(x):
    return 0.5 * x * (1.0 + lax.erf(x * (2.0 ** -0.5)))


def _sg_mixout_kernel(h_ref, u_ref, v_ref, gain_ref, ws_ref, bs_ref, yb_ref, w_ref, gn_ref,
                      o_ref, n_ref, w16_ref, ya_ref):
    @pl.when(pl.program_id(0) == 0)
    def _():
        w16_ref[...] = w_ref[...].astype(BF16)

    row = lax.broadcasted_iota(jnp.int32, (SG_CHUNK, SG_CHUNK), 0)
    col = lax.broadcasted_iota(jnp.int32, (SG_CHUNK, SG_CHUNK), 1)
    causal = row >= col
    acc = _dot(yb_ref[...], w16_ref[SG_WIDTH:, :])
    heads_per_dot = 2
    for h0 in range(0, SG_HEADS, heads_per_dot):
        for hd in range(h0, h0 + heads_per_dot):
            cols = slice(hd * SG_HEAD_DIM, (hd + 1) * SG_HEAD_DIM)
            w = jnp.where(causal, ws_ref[hd], 0.0).astype(BF16)
            bias = bs_ref[hd]
            gain = gain_ref[:, cols]
            for c in range(OUT_TM // SG_CHUNK):
                rows = slice(c * SG_CHUNK, (c + 1) * SG_CHUNK)
                u = _gelu(u_ref[rows, cols])
                v = _gelu(v_ref[rows, cols])
                vc = v - jnp.mean(v, axis=-1, keepdims=True)
                var = jnp.mean(vc * vc, axis=-1, keepdims=True)
                vn = vc * lax.rsqrt(var + EPS) * gain
                mixed = _dot(w, vn.astype(BF16)) + bias
                ya_ref[rows, cols] = (u * mixed).astype(BF16)
        kc = slice(h0 * SG_HEAD_DIM, (h0 + heads_per_dot) * SG_HEAD_DIM)
        acc += _dot(ya_ref[:, kc], w16_ref[kc, :])
    out = h_ref[...] + acc
    o_ref[...] = out
    n_ref[...] = _rms_norm_rows(out, gn_ref[...]).astype(n_ref.dtype)


def _sg_mixout(h, proj, v_gain, w_s, b_s, yb, w, g_next, layer):
    n_rows = h.shape[0]
    tm = OUT_TM
    return pl.pallas_call(
        _sg_mixout_kernel,
        grid=(n_rows // tm,),
        in_specs=[
            pl.BlockSpec((tm, D_MODEL), lambda i: (i, 0)),
            pl.BlockSpec((tm, SG_WIDTH), lambda i: (i, 0)),
            pl.BlockSpec((tm, SG_WIDTH), lambda i: (i, 1)),
            pl.BlockSpec((None, 1, SG_WIDTH), lambda i: (layer, 0, 0)),
            pl.BlockSpec((None, SG_HEADS, SG_CHUNK, SG_CHUNK), lambda i: (layer, 0, 0, 0)),
            pl.BlockSpec((None, SG_HEADS, SG_CHUNK, 1), lambda i: (layer, 0, 0, 0)),
            pl.BlockSpec((tm, GLA_VW), lambda i: (i, 0)),
            pl.BlockSpec((None, SG_WIDTH + GLA_VW, D_MODEL), lambda i: (layer, 0, 0),
                         pipeline_mode=pl.Buffered(1)),
            pl.BlockSpec((None, 1, D_MODEL), lambda i: (layer, 0, 0)),
        ],
        out_specs=[pl.BlockSpec((tm, D_MODEL), lambda i: (i, 0)),
                   pl.BlockSpec((tm, D_MODEL), lambda i: (i, 0))],
        out_shape=[jax.ShapeDtypeStruct((n_rows, D_MODEL), F32),
                   jax.ShapeDtypeStruct((n_rows, D_MODEL), BF16)],
        scratch_shapes=[pltpu.VMEM((SG_WIDTH + GLA_VW, D_MODEL), BF16),
                        pltpu.VMEM((tm, SG_WIDTH), BF16)],
        compiler_params=pltpu.CompilerParams(
            dimension_semantics=("arbitrary",),
            vmem_limit_bytes=VMEM_LIMIT),
        name="sg_mix_out",
    )(h, proj, proj, v_gain, w_s, b_s, yb, w, g_next)


def _split3(x):
    x1 = x.astype(BF16)
    r1 = x - x1.astype(F32)
    x2 = r1.astype(BF16)
    x3 = (r1 - x2.astype(F32)).astype(BF16)
    return x1, x2, x3


def _group_row(x, group, idx):
    n, d = x.shape
    x3 = x.reshape(n // group, group, d)
    return jnp.broadcast_to(x3[:, idx:idx + 1, :], x3.shape).reshape(n, d)


def _mixfront_kernel(h_ref, d_ref, g_ref, w_ref, wg_ref, bg_ref, og_ref, h1_ref, auv_ref, o_ref,
                     proj_ref, z_ref, proj_next_ref, z_next_ref, st_ref, *, tiles_per_seq):
    C = GLA_CHUNK
    step = pl.program_id(0)

    @pl.when(step == 0)
    def _():
        proj_ref[...] = jnp.zeros_like(proj_ref)
        z_ref[...] = jnp.zeros_like(z_ref)

    @pl.when((step == 0) | ((step - 1) % tiles_per_seq == 0))
    def _():
        st_ref[...] = jnp.zeros_like(st_ref)

    h1 = h_ref[...] + d_ref[...]
    h1_ref[...] = h1
    n = _rms_norm_rows(h1, g_ref[...]).astype(BF16)
    z_next_ref[...] = _dot(n, w_ref[:, MIX_MAIN:])
    n_slots = (MIX_TILE // C) * (GLA_HEADS + 1)
    n_slices = MIX_MAIN // MIX_SLICE
    slot_iter = iter(range(n_slots))

    def project_slice():
        slot = next(slot_iter)
        for i in range(-(-slot * n_slices // n_slots), -(-(slot + 1) * n_slices // n_slots)):
            lo = i * MIX_SLICE
            res = _dot(n, w_ref[:, lo:lo + MIX_SLICE])
            if lo < OFF_Q:
                auv_ref[:, lo:lo + MIX_SLICE] = res
            else:
                proj_next_ref[:, lo - OFF_Q:lo - OFF_Q + MIX_SLICE] = res

    pj = proj_ref
    zp = z_ref

    row = lax.broadcasted_iota(jnp.int32, (C, C), 0)
    col = lax.broadcasted_iota(jnp.int32, (C, C), 1)
    causal = row >= col
    tril = jnp.where(causal, 1.0, 0.0).astype(BF16)
    levels = []
    s = C // 2
    while s >= GLA_SUB:
        levels.append((s, ((row // s) % 2 == 1) & ((col // s) == (row // s) - 1)))
        s //= 2
    diag_off = col - (row // GLA_SUB) * GLA_SUB
    w_gate = wg_ref[...].astype(BF16)
    b_gate = bg_ref[...]
    g_scale = math.log2(math.e) / GLA_GATE_TAU

    for c in range(MIX_TILE // C):
        rows = slice(c * C, (c + 1) * C)
        project_slice()
        pre = _dot(zp[rows, :].astype(BF16), w_gate) + b_gate
        g = (jnp.minimum(pre, 0.0) - jnp.log1p(jnp.exp(-jnp.abs(pre)))) * g_scale
        g1, g2, g3 = _split3(g)
        b_all = _dot(tril, g1) + _dot(tril, g2) + _dot(tril, g3)

        for hd in range(GLA_HEADS):
            kc = slice(hd * GLA_DK, (hd + 1) * GLA_DK)
            vc = slice(hd * GLA_DV, (hd + 1) * GLA_DV)
            b = b_all[:, kc]
            project_slice()
            q = pj[rows, kc] * (GLA_DK ** -0.5)
            k = pj[rows, GLA_KW + hd * GLA_DK:GLA_KW + (hd + 1) * GLA_DK]
            v16 = pj[rows, 2 * GLA_KW + hd * GLA_DV:2 * GLA_KW + (hd + 1) * GLA_DV].astype(BF16)
            b_last = b[C - 1:C, :]

            st = st_ref[hd]
            o = _dot_nt((q * jnp.exp2(b)).astype(BF16), st.astype(BF16))

            sc = jnp.zeros((C, C), F32)
            for s, mask in levels:
                d = b - _group_row(b, 2 * s, s - 1)
                qs = q * jnp.exp2(jnp.minimum(d, 0.0))
                ks = k * jnp.exp2(jnp.minimum(-d, 0.0))
                sc = jnp.where(mask, _dot_nt(qs.astype(BF16), ks.astype(BF16)), sc)

            for j in range(GLA_SUB):
                e = jnp.exp2(b - _group_row(b, GLA_SUB, j))
                sj = jnp.sum(q * _group_row(k, GLA_SUB, j) * e, axis=-1, keepdims=True)
                sc = jnp.where(diag_off == j, sj, sc)
            sc = jnp.where(causal, sc, 0.0)
            o = o + _dot(sc.astype(BF16), v16)

            k_dec = k * jnp.exp2(b_last - b)
            st_ref[hd] = st * jnp.exp2(b_last) + _dot_tn(v16, k_dec.astype(BF16))

            ms = jnp.mean(o * o, axis=-1, keepdims=True)
            r = pj[rows, 2 * GLA_KW + GLA_VW + hd * GLA_DV:2 * GLA_KW + GLA_VW + (hd + 1) * GLA_DV]
            y = o * lax.rsqrt(ms + EPS) * og_ref[:, vc] * (r * jax.nn.sigmoid(r))
            o_ref[rows, vc] = y.astype(o_ref.dtype)

    proj_ref[...] = proj_next_ref[...]
    z_ref[...] = z_next_ref[...]


def _mixfront(h, d, g, w, w_gate, b_gate, o_gain, layer, t_len):
    n_rows = h.shape[0]
    tile = MIX_TILE
    n_tiles = n_rows // tile
    gla_cols = MIX_MAIN - OFF_Q
    return pl.pallas_call(
        functools.partial(_mixfront_kernel, tiles_per_seq=t_len // tile),
        grid=(n_tiles + 1,),
        in_specs=[
            pl.BlockSpec((tile, D_MODEL), lambda s: (jnp.minimum(s, n_tiles - 1), 0)),
            pl.BlockSpec((tile, D_MODEL), lambda s: (jnp.minimum(s, n_tiles - 1), 0)),
            pl.BlockSpec((None, 1, D_MODEL), lambda s: (layer, 0, 0)),
            pl.BlockSpec((None, D_MODEL, MIX_MAIN + GLA_GATE_RANK), lambda s: (layer, 0, 0),
                         pipeline_mode=pl.Buffered(1)),
            pl.BlockSpec((None, GLA_GATE_RANK, GLA_KW), lambda s: (layer, 0, 0)),
            pl.BlockSpec((None, 1, GLA_KW), lambda s: (layer, 0, 0)),
            pl.BlockSpec((None, 1, GLA_VW), lambda s: (layer, 0, 0)),
        ],
        out_specs=[
            pl.BlockSpec((tile, D_MODEL), lambda s: (jnp.minimum(s, n_tiles - 1), 0)),
            pl.BlockSpec((tile, OFF_Q), lambda s: (jnp.minimum(s, n_tiles - 1), 0)),
            pl.BlockSpec((tile, GLA_VW), lambda s: (jnp.maximum(s - 1, 0), 0)),
        ],
        out_shape=[
            jax.ShapeDtypeStruct((n_rows, D_MODEL), F32),
            jax.ShapeDtypeStruct((n_rows, OFF_Q), F32),
            jax.ShapeDtypeStruct((n_rows, GLA_VW), BF16),
        ],
        scratch_shapes=[
            pltpu.VMEM((tile, gla_cols), F32),
            pltpu.VMEM((tile, GLA_GATE_RANK), F32),
            pltpu.VMEM((tile, gla_cols), F32),
            pltpu.VMEM((tile, GLA_GATE_RANK), F32),
            pltpu.VMEM((GLA_HEADS, GLA_DV, GLA_DK), F32),
        ],
        compiler_params=pltpu.CompilerParams(
            dimension_semantics=("arbitrary",),
            vmem_limit_bytes=VMEM_LIMIT),
        name="mix_front",
    )(h, d, g, w, w_gate, b_gate, o_gain)


def _ple_kernel(h_ref, d_ref, g_ref, wg_ref, p_ref, wp_ref, gn_ref, *refs, final_norm):
    if final_norm:
        o_ref, wg16_ref, wp16_ref = refs
    else:
        o_ref, n_ref, wg16_ref, wp16_ref = refs

    @pl.when(pl.program_id(0) == 0)
    def _():
        wg16_ref[...] = wg_ref[...].astype(BF16)
        wp16_ref[...] = wp_ref[...].astype(BF16)

    h = h_ref[...] + d_ref[...]
    n = _rms_norm_rows(h, g_ref[...]).astype(BF16)
    gate = jax.nn.sigmoid(_dot(n, wg16_ref[...]))
    emb = _dot(p_ref[...].astype(BF16), wp16_ref[...])
    out = h + gate * emb
    if final_norm:
        o_ref[...] = _rms_norm_rows(out, gn_ref[...])
    else:
        o_ref[...] = out
        n_ref[...] = _rms_norm_rows(out, gn_ref[...]).astype(n_ref.dtype)


def _ple(h, d, g, w_gate, p, w_proj, g_next, layer, final_norm):
    n_rows = h.shape[0]
    tm = PLE_TM
    const = pl.Buffered(1)
    row_spec = pl.BlockSpec((tm, D_MODEL), lambda i: (i, 0))
    if final_norm:
        out_specs = row_spec
        out_shape = jax.ShapeDtypeStruct((n_rows, D_MODEL), F32)
    else:
        out_specs = [row_spec, row_spec]
        out_shape = [jax.ShapeDtypeStruct((n_rows, D_MODEL), F32),
                     jax.ShapeDtypeStruct((n_rows, D_MODEL), BF16)]
    return pl.pallas_call(
        functools.partial(_ple_kernel, final_norm=final_norm),
        grid=(n_rows // tm,),
        in_specs=[
            row_spec,
            row_spec,
            pl.BlockSpec((None, 1, D_MODEL), lambda i: (layer, 0, 0)),
            pl.BlockSpec((None, D_MODEL, D_MODEL), lambda i: (layer, 0, 0), pipeline_mode=const),
            pl.BlockSpec((None, tm, D_PLE), lambda i: (layer, i, 0)),
            pl.BlockSpec((None, D_PLE, D_MODEL), lambda i: (layer, 0, 0), pipeline_mode=const),
            pl.BlockSpec((1, D_MODEL), lambda i: (0, 0)),
        ],
        out_specs=out_specs,
        out_shape=out_shape,
        scratch_shapes=[pltpu.VMEM((D_MODEL, D_MODEL), BF16), pltpu.VMEM((D_PLE, D_MODEL), BF16)],
        compiler_params=pltpu.CompilerParams(
            dimension_semantics=("arbitrary",),
            vmem_limit_bytes=VMEM_LIMIT),
        name="ple",
    )(h, d, g, w_gate, p, w_proj, g_next)


def kernel(x, p, ffn1_norm, w_ffn1_in, w_ffn1_out, mix_norm, w_mix_in, sg_v_gain, sg_w, sg_b,
           gla_w_gate, gla_b_gate, gla_o_gain, w_mix_out, ffn2_norm, w_ffn2_in, w_ffn2_out,
           ple_norm, w_ple_gate, w_ple_proj, final_norm):
    bsz, t_len, d_model = x.shape
    depth = p.shape[0]
    n_rows = bsz * t_len
    h = x.reshape(n_rows, d_model)

    def row(v):
        return v.reshape(depth, 1, v.shape[-1])

    ffn1_g, mix_g, ffn2_g, ple_g = row(ffn1_norm), row(mix_norm), row(ffn2_norm), row(ple_norm)
    w_mi = w_mix_in.astype(BF16)
    p2 = p.reshape(depth, n_rows, D_PLE)
    sg_gain, sg_bias = row(sg_v_gain), sg_b.reshape(depth, SG_HEADS, SG_CHUNK, 1)
    gla_bg, gla_og = row(gla_b_gate), row(gla_o_gain)
    final_g = final_norm.reshape(1, d_model)

    n = _norm(h, ffn1_g, 0)
    for i in range(depth):
        last = i == depth - 1
        d = _ffn(n, w_ffn1_in, w_ffn1_out, i)
        h, a_uv, y_b = _mixfront(h, d, mix_g, w_mi, gla_w_gate, gla_bg, gla_og, i, t_len)
        h, n = _sg_mixout(h, a_uv, sg_gain, sg_w, sg_bias, y_b, w_mix_out, ffn2_g, i)
        d = _ffn(n, w_ffn2_in, w_ffn2_out, i)
        if last:
            h = _ple(h, d, ple_g, w_ple_gate, p2, w_ple_proj, final_g, i, final_norm=True)
        else:
            h, n = _ple(h, d, ple_g, w_ple_gate, p2, w_ple_proj, ffn1_norm[i + 1].reshape(1, d_model), i,
                        final_norm=False)
    return h.reshape(bsz, t_len, d_model)
```

```python
import functools
import math

import jax
import jax.numpy as jnp
from jax import lax
from jax.experimental import pallas as pl
from jax.experimental.pallas import tpu as pltpu

F32 = jnp.float32
BF16 = jnp.bfloat16

EPS = 1e-6
D_MODEL = 2048
D_FF = 5632
D_PLE = 256

SG_HEADS = 8
SG_HEAD_DIM = 128
SG_WIDTH = SG_HEADS * SG_HEAD_DIM
SG_CHUNK = 128

GLA_HEADS = 4
GLA_DK = 128
GLA_DV = 256
GLA_KW = GLA_HEADS * GLA_DK
GLA_VW = GLA_HEADS * GLA_DV
GLA_GATE_RANK = 16
GLA_GATE_TAU = 16.0
GLA_CHUNK = 64
GLA_SUB = 8

MIX_MAIN = 2 * SG_WIDTH + 2 * GLA_KW + 2 * GLA_VW
OFF_Q = 2 * SG_WIDTH

VMEM_LIMIT = 56 * 1024 * 1024

FFN_TM = 1024
FFN_TF = 512
MIX_TILE = 256
MIX_SLICE = 256
OUT_TM = 512
PLE_TM = 512


def _rms_norm_rows(h, g):
    ms = jnp.mean(h * h, axis=-1, keepdims=True)
    return h * lax.rsqrt(ms + EPS) * g


def _dot(a, b):
    return jnp.dot(a, b, preferred_element_type=F32)


def _dot_nt(a, b):
    return lax.dot_general(a, b, (((1,), (1,)), ((), ())), preferred_element_type=F32)


def _dot_tn(a, b):
    return lax.dot_general(a, b, (((0,), (0,)), ((), ())), preferred_element_type=F32)


def _norm_kernel(x_ref, g_ref, o_ref):
    o_ref[...] = _rms_norm_rows(x_ref[...], g_ref[...]).astype(o_ref.dtype)


def _norm(x, g, layer):
    n_rows = x.shape[0]
    tm = FFN_TM
    return pl.pallas_call(
        _norm_kernel,
        grid=(n_rows // tm,),
        in_specs=[
            pl.BlockSpec((tm, D_MODEL), lambda i: (i, 0)),
            pl.BlockSpec((None, 1, D_MODEL), lambda i: (layer, 0, 0)),
        ],
        out_specs=pl.BlockSpec((tm, D_MODEL), lambda i: (i, 0)),
        out_shape=jax.ShapeDtypeStruct((n_rows, D_MODEL), BF16),
        compiler_params=pltpu.CompilerParams(
            dimension_semantics=("parallel",),
            vmem_limit_bytes=VMEM_LIMIT),
        name="norm",
    )(x, g)


def _ffn_kernel(n_ref, wg_ref, wu_ref, wo_ref, o_ref):
    @pl.when(pl.program_id(1) == 0)
    def _():
        o_ref[...] = jnp.zeros_like(o_ref)

    n = n_ref[...]
    gate = _dot(n, wg_ref[...].astype(BF16))
    up = _dot(n, wu_ref[...].astype(BF16))
    act = (0.5 * gate) * jax.nn.sigmoid(gate) * up
    o_ref[...] += _dot(act.astype(BF16), wo_ref[...].astype(BF16))


def _ffn(n, w_in, w_out, layer):
    n_rows = n.shape[0]
    tm, tf = FFN_TM, FFN_TF
    nf = D_FF // tf
    return pl.pallas_call(
        _ffn_kernel,
        grid=(n_rows // tm, nf),
        in_specs=[
            pl.BlockSpec((tm, D_MODEL), lambda i, j: (i, 0)),
            pl.BlockSpec((None, D_MODEL, tf), lambda i, j: (layer, 0, j)),
            pl.BlockSpec((None, D_MODEL, tf), lambda i, j: (layer, 0, j + nf)),
            pl.BlockSpec((None, tf, D_MODEL), lambda i, j: (layer, j, 0)),
        ],
        out_specs=pl.BlockSpec((tm, D_MODEL), lambda i, j: (i, 0)),
        out_shape=jax.ShapeDtypeStruct((n_rows, D_MODEL), F32),
        compiler_params=pltpu.CompilerParams(
            dimension_semantics=("parallel", "arbitrary"),
            vmem_limit_bytes=VMEM_LIMIT),
        name="ffn",
    )(n, w_in, w_in, w_out)


def _gelu(x):
    return 0.5 * x * (1.0 + lax.erf(x * (2.0 ** -0.5)))


def _sg_mixout_kernel(h_ref, u_ref, v_ref, gain_ref, ws_ref, bs_ref, yb_ref, w_ref, gn_ref,
                      o_ref, n_ref, ya_ref):
    row = lax.broadcasted_iota(jnp.int32, (SG_CHUNK, SG_CHUNK), 0)
    col = lax.broadcasted_iota(jnp.int32, (SG_CHUNK, SG_CHUNK), 1)
    causal = row >= col
    acc = _dot(yb_ref[...], w_ref[SG_WIDTH:, :])
    heads_per_dot = 2
    for h0 in range(0, SG_HEADS, heads_per_dot):
        for hd in range(h0, h0 + heads_per_dot):
            cols = slice(hd * SG_HEAD_DIM, (hd + 1) * SG_HEAD_DIM)
            w = jnp.where(causal, ws_ref[hd], 0.0).astype(BF16)
            bias = bs_ref[hd]
            gain = gain_ref[:, cols]
            for c in range(OUT_TM // SG_CHUNK):
                rows = slice(c * SG_CHUNK, (c + 1) * SG_CHUNK)
                u = _gelu(u_ref[rows, cols])
                v = _gelu(v_ref[rows, cols])
                vc = v - jnp.mean(v, axis=-1, keepdims=True)
                var = jnp.mean(vc * vc, axis=-1, keepdims=True)
                vn = vc * lax.rsqrt(var + EPS) * gain
                mixed = _dot(w, vn.astype(BF16)) + bias
                ya_ref[rows, cols] = (u * mixed).astype(BF16)
        kc = slice(h0 * SG_HEAD_DIM, (h0 + heads_per_dot) * SG_HEAD_DIM)
        acc += _dot(ya_ref[:, kc], w_ref[kc, :])
    out = h_ref[...] + acc
    o_ref[...] = out
    n_ref[...] = _rms_norm_rows(out, gn_ref[...]).astype(n_ref.dtype)


def _sg_mixout(h, proj, v_gain, w_s, b_s, yb, w, g_next, layer):
    n_rows = h.shape[0]
    tm = OUT_TM
    return pl.pallas_call(
        _sg_mixout_kernel,
        grid=(n_rows // tm,),
        in_specs=[
            pl.BlockSpec((tm, D_MODEL), lambda i: (i, 0)),
            pl.BlockSpec((tm, SG_WIDTH), lambda i: (i, 0)),
            pl.BlockSpec((tm, SG_WIDTH), lambda i: (i, 1)),
            pl.BlockSpec((None, 1, SG_WIDTH), lambda i: (layer, 0, 0)),
            pl.BlockSpec((None, SG_HEADS, SG_CHUNK, SG_CHUNK), lambda i: (layer, 0, 0, 0)),
            pl.BlockSpec((None, SG_HEADS, SG_CHUNK, 1), lambda i: (layer, 0, 0, 0)),
            pl.BlockSpec((tm, GLA_VW), lambda i: (i, 0)),
            pl.BlockSpec((None, SG_WIDTH + GLA_VW, D_MODEL), lambda i: (layer, 0, 0),
                         pipeline_mode=pl.Buffered(1)),
            pl.BlockSpec((None, 1, D_MODEL), lambda i: (layer, 0, 0)),
        ],
        out_specs=[pl.BlockSpec((tm, D_MODEL), lambda i: (i, 0)),
                   pl.BlockSpec((tm, D_MODEL), lambda i: (i, 0))],
        out_shape=[jax.ShapeDtypeStruct((n_rows, D_MODEL), F32),
                   jax.ShapeDtypeStruct((n_rows, D_MODEL), BF16)],
        scratch_shapes=[pltpu.VMEM((tm, SG_WIDTH), BF16)],
        compiler_params=pltpu.CompilerParams(
            dimension_semantics=("parallel",),
            vmem_limit_bytes=VMEM_LIMIT),
        name="sg_mix_out",
    )(h, proj, proj, v_gain, w_s, b_s, yb, w, g_next)


def _split3(x):
    x1 = x.astype(BF16)
    r1 = x - x1.astype(F32)
    x2 = r1.astype(BF16)
    x3 = (r1 - x2.astype(F32)).astype(BF16)
    return x1, x2, x3


def _group_row(x, group, idx):
    n, d = x.shape
    x3 = x.reshape(n // group, group, d)
    return jnp.broadcast_to(x3[:, idx:idx + 1, :], x3.shape).reshape(n, d)


def _mixfront_kernel(h_ref, d_ref, g_ref, w_ref, wg_ref, bg_ref, og_ref, h1_ref, auv_ref, o_ref,
                     proj_ref, z_ref, proj_next_ref, z_next_ref, st_ref, *, tiles_per_seq):
    C = GLA_CHUNK
    step = pl.program_id(0)

    @pl.when(step == 0)
    def _():
        proj_ref[...] = jnp.zeros_like(proj_ref)
        z_ref[...] = jnp.zeros_like(z_ref)

    @pl.when((step == 0) | ((step - 1) % tiles_per_seq == 0))
    def _():
        st_ref[...] = jnp.zeros_like(st_ref)

    h1 = h_ref[...] + d_ref[...]
    h1_ref[...] = h1
    n = _rms_norm_rows(h1, g_ref[...]).astype(BF16)
    z_next_ref[...] = _dot(n, w_ref[:, MIX_MAIN:])
    n_slots = (MIX_TILE // C) * (GLA_HEADS + 1)
    n_slices = MIX_MAIN // MIX_SLICE
    slot_iter = iter(range(n_slots))

    def project_slice():
        slot = next(slot_iter)
        for i in range(-(-slot * n_slices // n_slots), -(-(slot + 1) * n_slices // n_slots)):
            lo = i * MIX_SLICE
            res = _dot(n, w_ref[:, lo:lo + MIX_SLICE])
            if lo < OFF_Q:
                auv_ref[:, lo:lo + MIX_SLICE] = res
            else:
                proj_next_ref[:, lo - OFF_Q:lo - OFF_Q + MIX_SLICE] = res

    pj = proj_ref
    zp = z_ref

    row = lax.broadcasted_iota(jnp.int32, (C, C), 0)
    col = lax.broadcasted_iota(jnp.int32, (C, C), 1)
    causal = row >= col
    tril = jnp.where(causal, 1.0, 0.0).astype(BF16)
    levels = []
    s = C // 2
    while s >= GLA_SUB:
        levels.append((s, ((row // s) % 2 == 1) & ((col // s) == (row // s) - 1)))
        s //= 2
    diag_off = col - (row // GLA_SUB) * GLA_SUB
    w_gate = wg_ref[...].astype(BF16)
    b_gate = bg_ref[...]
    g_scale = math.log2(math.e) / GLA_GATE_TAU

    for c in range(MIX_TILE // C):
        rows = slice(c * C, (c + 1) * C)
        project_slice()
        pre = _dot(zp[rows, :].astype(BF16), w_gate) + b_gate
        g = (jnp.minimum(pre, 0.0) - jnp.log1p(jnp.exp(-jnp.abs(pre)))) * g_scale
        g1, g2, g3 = _split3(g)
        b_all = _dot(tril, g1) + _dot(tril, g2) + _dot(tril, g3)

        for hd in range(GLA_HEADS):
            kc = slice(hd * GLA_DK, (hd + 1) * GLA_DK)
            vc = slice(hd * GLA_DV, (hd + 1) * GLA_DV)
            b = b_all[:, kc]
            project_slice()
            q = pj[rows, kc] * (GLA_DK ** -0.5)
            k = pj[rows, GLA_KW + hd * GLA_DK:GLA_KW + (hd + 1) * GLA_DK]
            v16 = pj[rows, 2 * GLA_KW + hd * GLA_DV:2 * GLA_KW + (hd + 1) * GLA_DV].astype(BF16)
            b_last = b[C - 1:C, :]

            st = st_ref[hd]
            o = _dot_nt((q * jnp.exp2(b)).astype(BF16), st.astype(BF16))

            sc = jnp.zeros((C, C), F32)
            for s, mask in levels:
                d = b - _group_row(b, 2 * s, s - 1)
                qs = q * jnp.exp2(jnp.minimum(d, 0.0))
                ks = k * jnp.exp2(jnp.minimum(-d, 0.0))
                sc = jnp.where(mask, _dot_nt(qs.astype(BF16), ks.astype(BF16)), sc)

            for j in range(GLA_SUB):
                e = jnp.exp2(b - _group_row(b, GLA_SUB, j))
                sj = jnp.sum(q * _group_row(k, GLA_SUB, j) * e, axis=-1, keepdims=True)
                sc = jnp.where(diag_off == j, sj, sc)
            sc = jnp.where(causal, sc, 0.0)
            o = o + _dot(sc.astype(BF16), v16)

            k_dec = k * jnp.exp2(b_last - b)
            st_ref[hd] = st * jnp.exp2(b_last) + _dot_tn(v16, k_dec.astype(BF16))

            ms = jnp.mean(o * o, axis=-1, keepdims=True)
            r = pj[rows, 2 * GLA_KW + GLA_VW + hd * GLA_DV:2 * GLA_KW + GLA_VW + (hd + 1) * GLA_DV]
            y = o * lax.rsqrt(ms + EPS) * og_ref[:, vc] * (r * jax.nn.sigmoid(r))
            o_ref[rows, vc] = y.astype(o_ref.dtype)

    proj_ref[...] = proj_next_ref[...]
    z_ref[...] = z_next_ref[...]


def _mixfront(h, d, g, w, w_gate, b_gate, o_gain, layer, t_len):
    n_rows = h.shape[0]
    tile = MIX_TILE
    n_tiles = n_rows // tile
    gla_cols = MIX_MAIN - OFF_Q
    return pl.pallas_call(
        functools.partial(_mixfront_kernel, tiles_per_seq=t_len // tile),
        grid=(n_tiles + 1,),
        in_specs=[
            pl.BlockSpec((tile, D_MODEL), lambda s: (jnp.minimum(s, n_tiles - 1), 0)),
            pl.BlockSpec((tile, D_MODEL), lambda s: (jnp.minimum(s, n_tiles - 1), 0)),
            pl.BlockSpec((None, 1, D_MODEL), lambda s: (layer, 0, 0)),
            pl.BlockSpec((None, D_MODEL, MIX_MAIN + GLA_GATE_RANK), lambda s: (layer, 0, 0),
                         pipeline_mode=pl.Buffered(1)),
            pl.BlockSpec((None, GLA_GATE_RANK, GLA_KW), lambda s: (layer, 0, 0)),
            pl.BlockSpec((None, 1, GLA_KW), lambda s: (layer, 0, 0)),
            pl.BlockSpec((None, 1, GLA_VW), lambda s: (layer, 0, 0)),
        ],
        out_specs=[
            pl.BlockSpec((tile, D_MODEL), lambda s: (jnp.minimum(s, n_tiles - 1), 0)),
            pl.BlockSpec((tile, OFF_Q), lambda s: (jnp.minimum(s, n_tiles - 1), 0)),
            pl.BlockSpec((tile, GLA_VW), lambda s: (jnp.maximum(s - 1, 0), 0)),
        ],
        out_shape=[
            jax.ShapeDtypeStruct((n_rows, D_MODEL), F32),
            jax.ShapeDtypeStruct((n_rows, OFF_Q), F32),
            jax.ShapeDtypeStruct((n_rows, GLA_VW), BF16),
        ],
        scratch_shapes=[
            pltpu.VMEM((tile, gla_cols), F32),
            pltpu.VMEM((tile, GLA_GATE_RANK), F32),
            pltpu.VMEM((tile, gla_cols), F32),
            pltpu.VMEM((tile, GLA_GATE_RANK), F32),
            pltpu.VMEM((GLA_HEADS, GLA_DV, GLA_DK), F32),
        ],
        compiler_params=pltpu.CompilerParams(
            dimension_semantics=("arbitrary",),
            vmem_limit_bytes=VMEM_LIMIT),
        name="mix_front",
    )(h, d, g, w, w_gate, b_gate, o_gain)


def _ple_kernel(h_ref, d_ref, g_ref, wg_ref, p_ref, wp_ref, gn_ref, *refs, final_norm):
    if final_norm:
        (o_ref,) = refs
    else:
        o_ref, n_ref = refs

    h = h_ref[...] + d_ref[...]
    n = _rms_norm_rows(h, g_ref[...]).astype(BF16)
    gate = jax.nn.sigmoid(_dot(n, wg_ref[...]))
    emb = _dot(p_ref[...].astype(BF16), wp_ref[...])
    out = h + gate * emb
    if final_norm:
        o_ref[...] = _rms_norm_rows(out, gn_ref[...])
    else:
        o_ref[...] = out
        n_ref[...] = _rms_norm_rows(out, gn_ref[...]).astype(n_ref.dtype)


def _ple(h, d, g, w_gate, p, w_proj, g_next, layer, final_norm):
    n_rows = h.shape[0]
    tm = PLE_TM
    const = pl.Buffered(1)
    row_spec = pl.BlockSpec((tm, D_MODEL), lambda i: (i, 0))
    if final_norm:
        out_specs = row_spec
        out_shape = jax.ShapeDtypeStruct((n_rows, D_MODEL), F32)
    else:
        out_specs = [row_spec, row_spec]
        out_shape = [jax.ShapeDtypeStruct((n_rows, D_MODEL), F32),
                     jax.ShapeDtypeStruct((n_rows, D_MODEL), BF16)]
    return pl.pallas_call(
        functools.partial(_ple_kernel, final_norm=final_norm),
        grid=(n_rows // tm,),
        in_specs=[
            row_spec,
            row_spec,
            pl.BlockSpec((None, 1, D_MODEL), lambda i: (layer, 0, 0)),
            pl.BlockSpec((None, D_MODEL, D_MODEL), lambda i: (layer, 0, 0), pipeline_mode=const),
            pl.BlockSpec((None, tm, D_PLE), lambda i: (layer, i, 0)),
            pl.BlockSpec((None, D_PLE, D_MODEL), lambda i: (layer, 0, 0), pipeline_mode=const),
            pl.BlockSpec((1, D_MODEL), lambda i: (0, 0)),
        ],
        out_specs=out_specs,
        out_shape=out_shape,
        compiler_params=pltpu.CompilerParams(
            dimension_semantics=("parallel",),
            vmem_limit_bytes=VMEM_LIMIT),
        name="ple",
    )(h, d, g, w_gate, p, w_proj, g_next)


def kernel(x, p, ffn1_norm, w_ffn1_in, w_ffn1_out, mix_norm, w_mix_in, sg_v_gain, sg_w, sg_b,
           gla_w_gate, gla_b_gate, gla_o_gain, w_mix_out, ffn2_norm, w_ffn2_in, w_ffn2_out,
           ple_norm, w_ple_gate, w_ple_proj, final_norm):
    bsz, t_len, d_model = x.shape
    depth = p.shape[0]
    n_rows = bsz * t_len
    h = x.reshape(n_rows, d_model)

    def row(v):
        return v.reshape(depth, 1, v.shape[-1])

    ffn1_g, mix_g, ffn2_g, ple_g = row(ffn1_norm), row(mix_norm), row(ffn2_norm), row(ple_norm)
    w_mi = w_mix_in.astype(BF16)
    w_mo, w_pg, w_pp = w_mix_out.astype(BF16), w_ple_gate.astype(BF16), w_ple_proj.astype(BF16)
    p2 = p.reshape(depth, n_rows, D_PLE)
    sg_gain, sg_bias = row(sg_v_gain), sg_b.reshape(depth, SG_HEADS, SG_CHUNK, 1)
    gla_bg, gla_og = row(gla_b_gate), row(gla_o_gain)
    final_g = final_norm.reshape(1, d_model)

    n = _norm(h, ffn1_g, 0)
    for i in range(depth):
        last = i == depth - 1
        d = _ffn(n, w_ffn1_in, w_ffn1_out, i)
        h, a_uv, y_b = _mixfront(h, d, mix_g, w_mi, gla_w_gate, gla_bg, gla_og, i, t_len)
        h, n = _sg_mixout(h, a_uv, sg_gain, sg_w, sg_bias, y_b, w_mo, ffn2_g, i)
        d = _ffn(n, w_ffn2_in, w_ffn2_out, i)
        if last:
            h = _ple(h, d, ple_g, w_pg, p2, w_pp, final_g, i, final_norm=True)
        else:
            h, n = _ple(h, d, ple_g, w_pg, p2, w_pp, ffn1_norm[i + 1].reshape(1, d_model), i,
                        final_norm=False)
    return h.reshape(bsz, t_len, d_model)
```

```python
import functools
import math

import jax
import jax.numpy as jnp
from jax import lax
from jax.experimental import pallas as pl
from jax.experimental.pallas import tpu as pltpu

F32 = jnp.float32
BF16 = jnp.bfloat16

EPS = 1e-6
D_MODEL = 2048
D_FF = 5632
D_PLE = 256

SG_HEADS = 8
SG_HEAD_DIM = 128
SG_WIDTH = SG_HEADS * SG_HEAD_DIM
SG_CHUNK = 128

GLA_HEADS = 4
GLA_DK = 128
GLA_DV = 256
GLA_KW = GLA_HEADS * GLA_DK
GLA_VW = GLA_HEADS * GLA_DV
GLA_GATE_RANK = 16
GLA_GATE_TAU = 16.0
GLA_CHUNK = 64
GLA_SUB = 8

MIX_MAIN = 2 * SG_WIDTH + 2 * GLA_KW + 2 * GLA_VW
OFF_Q = 2 * SG_WIDTH

VMEM_LIMIT = 56 * 1024 * 1024

NORM_TM = 2048
FFN_TM = 1024
FFN_TF = 512
MIX_TILE = 256
MIX_SLICE = 256
OUT_TM = 512
PLE_TM = 512


def _rms_norm_rows(h, g):
    ms = jnp.mean(h * h, axis=-1, keepdims=True)
    return h * lax.rsqrt(ms + EPS) * g


def _dot(a, b):
    return jnp.dot(a, b, preferred_element_type=F32)


def _dot_nt(a, b):
    return lax.dot_general(a, b, (((1,), (1,)), ((), ())), preferred_element_type=F32)


def _dot_tn(a, b):
    return lax.dot_general(a, b, (((0,), (0,)), ((), ())), preferred_element_type=F32)


def _norm_kernel(x_ref, g_ref, o_ref):
    o_ref[...] = _rms_norm_rows(x_ref[...], g_ref[...]).astype(o_ref.dtype)


def _norm(x, g, layer):
    n_rows = x.shape[0]
    tm = NORM_TM
    return pl.pallas_call(
        _norm_kernel,
        grid=(n_rows // tm,),
        in_specs=[
            pl.BlockSpec((tm, D_MODEL), lambda i: (i, 0)),
            pl.BlockSpec((None, 1, D_MODEL), lambda i: (layer, 0, 0)),
        ],
        out_specs=pl.BlockSpec((tm, D_MODEL), lambda i: (i, 0)),
        out_shape=jax.ShapeDtypeStruct((n_rows, D_MODEL), BF16),
        compiler_params=pltpu.CompilerParams(
            dimension_semantics=("parallel",),
            vmem_limit_bytes=VMEM_LIMIT),
        name="norm",
    )(x, g)


def _ffn_kernel(n_ref, wg_ref, wu_ref, wo_ref, o_ref):
    def partial_sum():
        n = n_ref[...]
        gate = _dot(n, wg_ref[...].astype(BF16))
        up = _dot(n, wu_ref[...].astype(BF16))
        act = (0.5 * gate) * jax.nn.sigmoid(gate) * up
        return _dot(act.astype(BF16), wo_ref[...].astype(BF16))

    @pl.when(pl.program_id(1) == 0)
    def _():
        o_ref[...] = partial_sum()

    @pl.when(pl.program_id(1) != 0)
    def _():
        o_ref[...] += partial_sum()


def _ffn(n, w_in, w_out, layer):
    n_rows = n.shape[0]
    tm, tf = FFN_TM, FFN_TF
    nf = D_FF // tf
    return pl.pallas_call(
        _ffn_kernel,
        grid=(n_rows // tm, nf),
        in_specs=[
            pl.BlockSpec((tm, D_MODEL), lambda i, j: (i, 0)),
            pl.BlockSpec((None, D_MODEL, tf), lambda i, j: (layer, 0, j)),
            pl.BlockSpec((None, D_MODEL, tf), lambda i, j: (layer, 0, j + nf)),
            pl.BlockSpec((None, tf, D_MODEL), lambda i, j: (layer, j, 0)),
        ],
        out_specs=pl.BlockSpec((tm, D_MODEL), lambda i, j: (i, 0)),
        out_shape=jax.ShapeDtypeStruct((n_rows, D_MODEL), F32),
        compiler_params=pltpu.CompilerParams(
            dimension_semantics=("parallel", "arbitrary"),
            vmem_limit_bytes=VMEM_LIMIT),
        name="ffn",
    )(n, w_in, w_in, w_out)


def _gelu(x):
    return 0.5 * x * (1.0 + lax.erf(x * (2.0 ** -0.5)))


def _sg_mixout_kernel(h_ref, u_ref, v_ref, gain_ref, ws_ref, bs_ref, yb_ref, w_ref, gn_ref,
                      o_ref, n_ref, ya_ref):
    row = lax.broadcasted_iota(jnp.int32, (SG_CHUNK, SG_CHUNK), 0)
    col = lax.broadcasted_iota(jnp.int32, (SG_CHUNK, SG_CHUNK), 1)
    causal = row >= col
    acc = _dot(yb_ref[...], w_ref[SG_WIDTH:, :])
    heads_per_dot = 2
    for h0 in range(0, SG_HEADS, heads_per_dot):
        for hd in range(h0, h0 + heads_per_dot):
            cols = slice(hd * SG_HEAD_DIM, (hd + 1) * SG_HEAD_DIM)
            w = jnp.where(causal, ws_ref[hd], 0.0).astype(BF16)
            bias = bs_ref[hd]
            gain = gain_ref[:, cols]
            for c in range(OUT_TM // SG_CHUNK):
                rows = slice(c * SG_CHUNK, (c + 1) * SG_CHUNK)
                u = _gelu(u_ref[rows, cols])
                v = _gelu(v_ref[rows, cols])
                vc = v - jnp.mean(v, axis=-1, keepdims=True)
                var = jnp.mean(vc * vc, axis=-1, keepdims=True)
                vn = vc * lax.rsqrt(var + EPS) * gain
                mixed = _dot(w, vn.astype(BF16)) + bias
                ya_ref[rows, cols] = (u * mixed).astype(BF16)
        kc = slice(h0 * SG_HEAD_DIM, (h0 + heads_per_dot) * SG_HEAD_DIM)
        acc += _dot(ya_ref[:, kc], w_ref[kc, :])
    out = h_ref[...] + acc
    o_ref[...] = out
    n_ref[...] = _rms_norm_rows(out, gn_ref[...]).astype(n_ref.dtype)


def _sg_mixout(h, proj, v_gain, w_s, b_s, yb, w, g_next, layer):
    n_rows = h.shape[0]
    tm = OUT_TM
    return pl.pallas_call(
        _sg_mixout_kernel,
        grid=(n_rows // tm,),
        in_specs=[
            pl.BlockSpec((tm, D_MODEL), lambda i: (i, 0)),
            pl.BlockSpec((tm, SG_WIDTH), lambda i: (i, 0)),
            pl.BlockSpec((tm, SG_WIDTH), lambda i: (i, 1)),
            pl.BlockSpec((None, 1, SG_WIDTH), lambda i: (layer, 0, 0)),
            pl.BlockSpec((None, SG_HEADS, SG_CHUNK, SG_CHUNK), lambda i: (layer, 0, 0, 0)),
            pl.BlockSpec((None, SG_HEADS, SG_CHUNK, 1), lambda i: (layer, 0, 0, 0)),
            pl.BlockSpec((tm, GLA_VW), lambda i: (i, 0)),
            pl.BlockSpec((SG_WIDTH + GLA_VW, D_MODEL), lambda i: (0, 0), pipeline_mode=pl.Buffered(1)),
            pl.BlockSpec((None, 1, D_MODEL), lambda i: (layer, 0, 0)),
        ],
        out_specs=[pl.BlockSpec((tm, D_MODEL), lambda i: (i, 0)),
                   pl.BlockSpec((tm, D_MODEL), lambda i: (i, 0))],
        out_shape=[jax.ShapeDtypeStruct((n_rows, D_MODEL), F32),
                   jax.ShapeDtypeStruct((n_rows, D_MODEL), BF16)],
        scratch_shapes=[pltpu.VMEM((tm, SG_WIDTH), BF16)],
        compiler_params=pltpu.CompilerParams(
            dimension_semantics=("parallel",),
            vmem_limit_bytes=VMEM_LIMIT),
        name="sg_mix_out",
    )(h, proj, proj, v_gain, w_s, b_s, yb, w, g_next)


def _split3(x):
    x1 = x.astype(BF16)
    r1 = x - x1.astype(F32)
    x2 = r1.astype(BF16)
    x3 = (r1 - x2.astype(F32)).astype(BF16)
    return x1, x2, x3


def _group_row(x, group, idx):
    n, d = x.shape
    x3 = x.reshape(n // group, group, d)
    return jnp.broadcast_to(x3[:, idx:idx + 1, :], x3.shape).reshape(n, d)


def _mixfront_kernel(h_ref, d_ref, g_ref, w_ref, wg_ref, bg_ref, og_ref, wa_ref, wb_ref,
                     h1_ref, auv_ref, o_ref, wa16_ref, wb16_ref,
                     proj_ref, z_ref, proj_next_ref, z_next_ref, st_ref, *, tiles_per_seq):
    C = GLA_CHUNK
    step = pl.program_id(0)

    @pl.when(step == 0)
    def _():
        proj_ref[...] = jnp.zeros_like(proj_ref)
        z_ref[...] = jnp.zeros_like(z_ref)

    @pl.when((step == 0) | ((step - 1) % tiles_per_seq == 0))
    def _():
        st_ref[...] = jnp.zeros_like(st_ref)

    wa16_ref[...] = wa_ref[...].astype(BF16)
    wb16_ref[...] = wb_ref[...].astype(BF16)

    h1 = h_ref[...] + d_ref[...]
    h1_ref[...] = h1
    n = _rms_norm_rows(h1, g_ref[...]).astype(BF16)
    z_next_ref[...] = _dot(n, w_ref[:, MIX_MAIN:])
    n_slots = (MIX_TILE // C) * (GLA_HEADS + 1)
    n_slices = MIX_MAIN // MIX_SLICE
    slot_iter = iter(range(n_slots))

    def project_slice():
        slot = next(slot_iter)
        for i in range(-(-slot * n_slices // n_slots), -(-(slot + 1) * n_slices // n_slots)):
            lo = i * MIX_SLICE
            res = _dot(n, w_ref[:, lo:lo + MIX_SLICE])
            if lo < OFF_Q:
                auv_ref[:, lo:lo + MIX_SLICE] = res
            else:
                proj_next_ref[:, lo - OFF_Q:lo - OFF_Q + MIX_SLICE] = res

    pj = proj_ref
    zp = z_ref

    row = lax.broadcasted_iota(jnp.int32, (C, C), 0)
    col = lax.broadcasted_iota(jnp.int32, (C, C), 1)
    causal = row >= col
    tril = jnp.where(causal, 1.0, 0.0).astype(BF16)
    levels = []
    s = C // 2
    while s >= GLA_SUB:
        levels.append((s, ((row // s) % 2 == 1) & ((col // s) == (row // s) - 1)))
        s //= 2
    diag_off = col - (row // GLA_SUB) * GLA_SUB
    w_gate = wg_ref[...].astype(BF16)
    b_gate = bg_ref[...]
    g_scale = math.log2(math.e) / GLA_GATE_TAU

    for c in range(MIX_TILE // C):
        rows = slice(c * C, (c + 1) * C)
        project_slice()
        pre = _dot(zp[rows, :].astype(BF16), w_gate) + b_gate
        g = (jnp.minimum(pre, 0.0) - jnp.log1p(jnp.exp(-jnp.abs(pre)))) * g_scale
        g1, g2, g3 = _split3(g)
        b_all = _dot(tril, g1) + _dot(tril, g2) + _dot(tril, g3)

        for hd in range(GLA_HEADS):
            kc = slice(hd * GLA_DK, (hd + 1) * GLA_DK)
            vc = slice(hd * GLA_DV, (hd + 1) * GLA_DV)
            b = b_all[:, kc]
            project_slice()
            q = pj[rows, kc] * (GLA_DK ** -0.5)
            k = pj[rows, GLA_KW + hd * GLA_DK:GLA_KW + (hd + 1) * GLA_DK]
            v16 = pj[rows, 2 * GLA_KW + hd * GLA_DV:2 * GLA_KW + (hd + 1) * GLA_DV].astype(BF16)
            b_last = b[C - 1:C, :]

            st = st_ref[hd]
            o = _dot_nt((q * jnp.exp2(b)).astype(BF16), st.astype(BF16))

            sc = jnp.zeros((C, C), F32)
            for s, mask in levels:
                d = b - _group_row(b, 2 * s, s - 1)
                qs = q * jnp.exp2(jnp.minimum(d, 0.0))
                ks = k * jnp.exp2(jnp.minimum(-d, 0.0))
                sc = jnp.where(mask, _dot_nt(qs.astype(BF16), ks.astype(BF16)), sc)

            for j in range(GLA_SUB):
                e = jnp.exp2(b - _group_row(b, GLA_SUB, j))
                sj = jnp.sum(q * _group_row(k, GLA_SUB, j) * e, axis=-1, keepdims=True)
                sc = jnp.where(diag_off == j, sj, sc)
            sc = jnp.where(causal, sc, 0.0)
            o = o + _dot(sc.astype(BF16), v16)

            k_dec = k * jnp.exp2(b_last - b)
            st_ref[hd] = st * jnp.exp2(b_last) + _dot_tn(v16, k_dec.astype(BF16))

            ms = jnp.mean(o * o, axis=-1, keepdims=True)
            r = pj[rows, 2 * GLA_KW + GLA_VW + hd * GLA_DV:2 * GLA_KW + GLA_VW + (hd + 1) * GLA_DV]
            y = o * lax.rsqrt(ms + EPS) * og_ref[:, vc] * (r * jax.nn.sigmoid(r))
            o_ref[rows, vc] = y.astype(o_ref.dtype)

    proj_ref[...] = proj_next_ref[...]
    z_ref[...] = z_next_ref[...]


def _mixfront(h, d, g, w, w_gate, b_gate, o_gain, w_a, w_b, layer, t_len):
    n_rows = h.shape[0]
    tile = MIX_TILE
    n_tiles = n_rows // tile
    gla_cols = MIX_MAIN - OFF_Q
    w_rows = D_MODEL // n_tiles
    row_tile = lambda s: (jnp.minimum(s, n_tiles - 1), 0)
    return pl.pallas_call(
        functools.partial(_mixfront_kernel, tiles_per_seq=t_len // tile),
        grid=(n_tiles + 1,),
        in_specs=[
            pl.BlockSpec((tile, D_MODEL), lambda s: (jnp.minimum(s, n_tiles - 1), 0)),
            pl.BlockSpec((tile, D_MODEL), lambda s: (jnp.minimum(s, n_tiles - 1), 0)),
            pl.BlockSpec((None, 1, D_MODEL), lambda s: (layer, 0, 0)),
            pl.BlockSpec((None, D_MODEL, MIX_MAIN + GLA_GATE_RANK), lambda s: (layer, 0, 0),
                         pipeline_mode=pl.Buffered(1)),
            pl.BlockSpec((None, GLA_GATE_RANK, GLA_KW), lambda s: (layer, 0, 0)),
            pl.BlockSpec((None, 1, GLA_KW), lambda s: (layer, 0, 0)),
            pl.BlockSpec((None, 1, GLA_VW), lambda s: (layer, 0, 0)),
            pl.BlockSpec((None, w_rows, D_MODEL), lambda s: (layer,) + row_tile(s)),
            pl.BlockSpec((None, w_rows, D_MODEL), lambda s: (layer,) + row_tile(s)),
        ],
        out_specs=[
            pl.BlockSpec((tile, D_MODEL), lambda s: (jnp.minimum(s, n_tiles - 1), 0)),
            pl.BlockSpec((tile, OFF_Q), lambda s: (jnp.minimum(s, n_tiles - 1), 0)),
            pl.BlockSpec((tile, GLA_VW), lambda s: (jnp.maximum(s - 1, 0), 0)),
            pl.BlockSpec((w_rows, D_MODEL), row_tile),
            pl.BlockSpec((w_rows, D_MODEL), row_tile),
        ],
        out_shape=[
            jax.ShapeDtypeStruct((n_rows, D_MODEL), F32),
            jax.ShapeDtypeStruct((n_rows, OFF_Q), F32),
            jax.ShapeDtypeStruct((n_rows, GLA_VW), BF16),
            jax.ShapeDtypeStruct((D_MODEL, D_MODEL), BF16),
            jax.ShapeDtypeStruct((D_MODEL, D_MODEL), BF16),
        ],
        scratch_shapes=[
            pltpu.VMEM((tile, gla_cols), F32),
            pltpu.VMEM((tile, GLA_GATE_RANK), F32),
            pltpu.VMEM((tile, gla_cols), F32),
            pltpu.VMEM((tile, GLA_GATE_RANK), F32),
            pltpu.VMEM((GLA_HEADS, GLA_DV, GLA_DK), F32),
        ],
        compiler_params=pltpu.CompilerParams(
            dimension_semantics=("arbitrary",),
            vmem_limit_bytes=VMEM_LIMIT),
        name="mix_front",
    )(h, d, g, w, w_gate, b_gate, o_gain, w_a, w_b)


def _ple_kernel(h_ref, d_ref, g_ref, wg_ref, p_ref, wp_ref, gn_ref, *refs, final_norm):
    if final_norm:
        (o_ref,) = refs
    else:
        o_ref, n_ref = refs

    h = h_ref[...] + d_ref[...]
    n = _rms_norm_rows(h, g_ref[...]).astype(BF16)
    gate = jax.nn.sigmoid(_dot(n, wg_ref[...]))
    emb = _dot(p_ref[...].astype(BF16), wp_ref[...])
    out = h + gate * emb
    if final_norm:
        o_ref[...] = _rms_norm_rows(out, gn_ref[...])
    else:
        o_ref[...] = out
        n_ref[...] = _rms_norm_rows(out, gn_ref[...]).astype(n_ref.dtype)


def _ple(h, d, g, w_gate, p, w_proj, g_next, layer, final_norm):
    n_rows = h.shape[0]
    tm = PLE_TM
    const = pl.Buffered(1)
    row_spec = pl.BlockSpec((tm, D_MODEL), lambda i: (i, 0))
    if final_norm:
        out_specs = row_spec
        out_shape = jax.ShapeDtypeStruct((n_rows, D_MODEL), F32)
    else:
        out_specs = [row_spec, row_spec]
        out_shape = [jax.ShapeDtypeStruct((n_rows, D_MODEL), F32),
                     jax.ShapeDtypeStruct((n_rows, D_MODEL), BF16)]
    return pl.pallas_call(
        functools.partial(_ple_kernel, final_norm=final_norm),
        grid=(n_rows // tm,),
        in_specs=[
            row_spec,
            row_spec,
            pl.BlockSpec((None, 1, D_MODEL), lambda i: (layer, 0, 0)),
            pl.BlockSpec((D_MODEL, D_MODEL), lambda i: (0, 0), pipeline_mode=const),
            pl.BlockSpec((None, tm, D_PLE), lambda i: (layer, i, 0)),
            pl.BlockSpec((None, D_PLE, D_MODEL), lambda i: (layer, 0, 0), pipeline_mode=const),
            pl.BlockSpec((1, D_MODEL), lambda i: (0, 0)),
        ],
        out_specs=out_specs,
        out_shape=out_shape,
        compiler_params=pltpu.CompilerParams(
            dimension_semantics=("parallel",),
            vmem_limit_bytes=VMEM_LIMIT),
        name="ple",
    )(h, d, g, w_gate, p, w_proj, g_next)


def kernel(x, p, ffn1_norm, w_ffn1_in, w_ffn1_out, mix_norm, w_mix_in, sg_v_gain, sg_w, sg_b,
           gla_w_gate, gla_b_gate, gla_o_gain, w_mix_out, ffn2_norm, w_ffn2_in, w_ffn2_out,
           ple_norm, w_ple_gate, w_ple_proj, final_norm):
    bsz, t_len, d_model = x.shape
    depth = p.shape[0]
    n_rows = bsz * t_len
    h = x.reshape(n_rows, d_model)

    def row(v):
        return v.reshape(depth, 1, v.shape[-1])

    ffn1_g, mix_g, ffn2_g, ple_g = row(ffn1_norm), row(mix_norm), row(ffn2_norm), row(ple_norm)
    w_mi = w_mix_in.astype(BF16)
    w_pp = w_ple_proj.astype(BF16)
    p2 = p.reshape(depth, n_rows, D_PLE)
    sg_gain, sg_bias = row(sg_v_gain), sg_b.reshape(depth, SG_HEADS, SG_CHUNK, 1)
    gla_bg, gla_og = row(gla_b_gate), row(gla_o_gain)
    final_g = final_norm.reshape(1, d_model)

    n = _norm(h, ffn1_g, 0)
    for i in range(depth):
        last = i == depth - 1
        d = _ffn(n, w_ffn1_in, w_ffn1_out, i)
        h, a_uv, y_b, w_mo, w_pg = _mixfront(h, d, mix_g, w_mi, gla_w_gate, gla_bg, gla_og,
                                             w_mix_out, w_ple_gate, i, t_len)
        h, n = _sg_mixout(h, a_uv, sg_gain, sg_w, sg_bias, y_b, w_mo, ffn2_g, i)
        d = _ffn(n, w_ffn2_in, w_ffn2_out, i)
        if last:
            h = _ple(h, d, ple_g, w_pg, p2, w_pp, final_g, i, final_norm=True)
        else:
            h, n = _ple(h, d, ple_g, w_pg, p2, w_pp, ffn1_norm[i + 1].reshape(1, d_model), i,
                        final_norm=False)
    return h.reshape(bsz, t_len, d_model)
```

```python
import functools
import math

import jax
import jax.numpy as jnp
from jax import lax
from jax.experimental import pallas as pl
from jax.experimental.pallas import tpu as pltpu

F32 = jnp.float32
BF16 = jnp.bfloat16

EPS = 1e-6
D_MODEL = 2048
D_FF = 5632
D_PLE = 256

SG_HEADS = 8
SG_HEAD_DIM = 128
SG_WIDTH = SG_HEADS * SG_HEAD_DIM
SG_CHUNK = 128

GLA_HEADS = 4
GLA_DK = 128
GLA_DV = 256
GLA_KW = GLA_HEADS * GLA_DK
GLA_VW = GLA_HEADS * GLA_DV
GLA_GATE_RANK = 16
GLA_GATE_TAU = 16.0
GLA_CHUNK = 64
GLA_SUB = 8

MIX_MAIN = 2 * SG_WIDTH + 2 * GLA_KW + 2 * GLA_VW
OFF_Q = 2 * SG_WIDTH

VMEM_LIMIT = 56 * 1024 * 1024

FFN_TM = 1024
FFN_TF = 512
FFN_RING = 3
ROW_RING = 3
FFN_VMEM_LIMIT = 63 * 1024 * 1024
MIX_TILE = 256
MIX_SLICE = 256
OUT_TM = 512
PLE_TM = 512


def _rms_norm_rows(h, g):
    ms = jnp.mean(h * h, axis=-1, keepdims=True)
    return h * lax.rsqrt(ms + EPS) * g


def _dot(a, b):
    return jnp.dot(a, b, preferred_element_type=F32)


def _dot_nt(a, b):
    return lax.dot_general(a, b, (((1,), (1,)), ((), ())), preferred_element_type=F32)


def _ring_rows(src_hbm, ring_ref, sem_ref):
    rows = ring_ref.shape[1]
    step = pl.program_id(0)

    def row_copy(u, slot):
        return pltpu.make_async_copy(src_hbm.at[pl.ds(pl.multiple_of(u * rows, rows), rows), :],
                                     ring_ref.at[slot], sem_ref.at[slot])

    @pl.when(step == 0)
    def _():
        for u in range(ROW_RING - 1):
            row_copy(u, u).start()

    ahead = step + (ROW_RING - 1)

    @pl.when(ahead < pl.num_programs(0))
    def _():
        row_copy(ahead, lax.rem(ahead, ROW_RING)).start()

    slot = lax.rem(step, ROW_RING)
    row_copy(step, slot).wait()
    return slot


def _dot_tn(a, b):
    return lax.dot_general(a, b, (((0,), (0,)), ((), ())), preferred_element_type=F32)


def _norm_kernel(x_ref, g_ref, o_ref):
    o_ref[...] = _rms_norm_rows(x_ref[...], g_ref[...]).astype(o_ref.dtype)


def _norm(x, g, layer):
    n_rows = x.shape[0]
    tm = FFN_TM
    return pl.pallas_call(
        _norm_kernel,
        grid=(n_rows // tm,),
        in_specs=[
            pl.BlockSpec((tm, D_MODEL), lambda i: (i, 0)),
            pl.BlockSpec((None, 1, D_MODEL), lambda i: (layer, 0, 0)),
        ],
        out_specs=pl.BlockSpec((tm, D_MODEL), lambda i: (i, 0)),
        out_shape=jax.ShapeDtypeStruct((n_rows, D_MODEL), BF16),
        compiler_params=pltpu.CompilerParams(
            dimension_semantics=("parallel",),
            vmem_limit_bytes=VMEM_LIMIT),
        name="norm",
    )(x, g)


def _ffn_kernel(n_ref, win_hbm, wo_ref, o_ref, ring_ref, sem_ref, *, layer, n_steps):
    tf = FFN_TF
    nf = pl.num_programs(1)
    step = pl.program_id(0) * nf + pl.program_id(1)

    def tile_copies(u, slot):
        j = lax.rem(u, nf)
        return [
            pltpu.make_async_copy(
                win_hbm.at[layer, :, pl.ds(pl.multiple_of((j + half * nf) * tf, tf), tf)],
                ring_ref.at[slot, half], sem_ref.at[slot, half])
            for half in range(2)]

    @pl.when(step == 0)
    def _():
        for u in range(FFN_RING - 1):
            for cp in tile_copies(u, u):
                cp.start()

    ahead = step + (FFN_RING - 1)

    @pl.when(ahead < n_steps)
    def _():
        for cp in tile_copies(ahead, lax.rem(ahead, FFN_RING)):
            cp.start()

    slot = lax.rem(step, FFN_RING)
    for cp in tile_copies(step, slot):
        cp.wait()

    @pl.when(pl.program_id(1) == 0)
    def _():
        o_ref[...] = jnp.zeros_like(o_ref)

    n = n_ref[...]
    gate = _dot(n, ring_ref[slot, 0].astype(BF16))
    up = _dot(n, ring_ref[slot, 1].astype(BF16))
    act = (0.5 * gate) * jax.nn.sigmoid(gate) * up
    o_ref[...] += _dot(act.astype(BF16), wo_ref[...].astype(BF16))


def _ffn(n, w_in, w_out, layer):
    n_rows = n.shape[0]
    tm, tf = FFN_TM, FFN_TF
    nf = D_FF // tf
    n_steps = (n_rows // tm) * nf
    return pl.pallas_call(
        functools.partial(_ffn_kernel, layer=layer, n_steps=n_steps),
        grid=(n_rows // tm, nf),
        in_specs=[
            pl.BlockSpec((tm, D_MODEL), lambda i, j: (i, 0)),
            pl.BlockSpec(memory_space=pl.ANY),
            pl.BlockSpec((None, tf, D_MODEL), lambda i, j: (layer, j, 0)),
        ],
        out_specs=pl.BlockSpec((tm, D_MODEL), lambda i, j: (i, 0)),
        out_shape=jax.ShapeDtypeStruct((n_rows, D_MODEL), F32),
        scratch_shapes=[pltpu.VMEM((FFN_RING, 2, D_MODEL, tf), F32),
                        pltpu.SemaphoreType.DMA((FFN_RING, 2))],
        compiler_params=pltpu.CompilerParams(
            dimension_semantics=("arbitrary", "arbitrary"),
            vmem_limit_bytes=FFN_VMEM_LIMIT),
        name="ffn",
    )(n, w_in, w_out)


def _gelu(x):
    return 0.5 * x * (1.0 + lax.erf(x * (2.0 ** -0.5)))


def _sg_mixout_kernel(h_hbm, u_ref, v_ref, gain_ref, ws_ref, bs_ref, yb_ref, w_ref, gn_ref,
                      o_ref, n_ref, ya_ref, hring_ref, hsem_ref):
    h_slot = _ring_rows(h_hbm, hring_ref, hsem_ref)
    row = lax.broadcasted_iota(jnp.int32, (SG_CHUNK, SG_CHUNK), 0)
    col = lax.broadcasted_iota(jnp.int32, (SG_CHUNK, SG_CHUNK), 1)
    causal = row >= col
    acc = _dot(yb_ref[...], w_ref[SG_WIDTH:, :])
    heads_per_dot = 2
    for h0 in range(0, SG_HEADS, heads_per_dot):
        for hd in range(h0, h0 + heads_per_dot):
            cols = slice(hd * SG_HEAD_DIM, (hd + 1) * SG_HEAD_DIM)
            w = jnp.where(causal, ws_ref[hd], 0.0).astype(BF16)
            bias = bs_ref[hd]
            gain = gain_ref[:, cols]
            for c in range(OUT_TM // SG_CHUNK):
                rows = slice(c * SG_CHUNK, (c + 1) * SG_CHUNK)
                u = _gelu(u_ref[rows, cols])
                v = _gelu(v_ref[rows, cols])
                vc = v - jnp.mean(v, axis=-1, keepdims=True)
                var = jnp.mean(vc * vc, axis=-1, keepdims=True)
                vn = vc * lax.rsqrt(var + EPS) * gain
                mixed = _dot(w, vn.astype(BF16)) + bias
                ya_ref[rows, cols] = (u * mixed).astype(BF16)
        kc = slice(h0 * SG_HEAD_DIM, (h0 + heads_per_dot) * SG_HEAD_DIM)
        acc += _dot(ya_ref[:, kc], w_ref[kc, :])
    out = hring_ref[h_slot] + acc
    o_ref[...] = out
    n_ref[...] = _rms_norm_rows(out, gn_ref[...]).astype(n_ref.dtype)


def _sg_mixout(h, proj, v_gain, w_s, b_s, yb, w, g_next, layer):
    n_rows = h.shape[0]
    tm = OUT_TM
    return pl.pallas_call(
        _sg_mixout_kernel,
        grid=(n_rows // tm,),
        in_specs=[
            pl.BlockSpec(memory_space=pl.ANY),
            pl.BlockSpec((tm, SG_WIDTH), lambda i: (i, 0)),
            pl.BlockSpec((tm, SG_WIDTH), lambda i: (i, 1)),
            pl.BlockSpec((None, 1, SG_WIDTH), lambda i: (layer, 0, 0)),
            pl.BlockSpec((None, SG_HEADS, SG_CHUNK, SG_CHUNK), lambda i: (layer, 0, 0, 0)),
            pl.BlockSpec((None, SG_HEADS, SG_CHUNK, 1), lambda i: (layer, 0, 0, 0)),
            pl.BlockSpec((tm, GLA_VW), lambda i: (i, 0)),
            pl.BlockSpec((SG_WIDTH + GLA_VW, D_MODEL), lambda i: (0, 0), pipeline_mode=pl.Buffered(1)),
            pl.BlockSpec((None, 1, D_MODEL), lambda i: (layer, 0, 0)),
        ],
        out_specs=[pl.BlockSpec((tm, D_MODEL), lambda i: (i, 0)),
                   pl.BlockSpec((tm, D_MODEL), lambda i: (i, 0))],
        out_shape=[jax.ShapeDtypeStruct((n_rows, D_MODEL), F32),
                   jax.ShapeDtypeStruct((n_rows, D_MODEL), BF16)],
        scratch_shapes=[pltpu.VMEM((tm, SG_WIDTH), BF16),
                        pltpu.VMEM((ROW_RING, tm, D_MODEL), F32),
                        pltpu.SemaphoreType.DMA((ROW_RING,))],
        compiler_params=pltpu.CompilerParams(
            dimension_semantics=("arbitrary",),
            vmem_limit_bytes=VMEM_LIMIT),
        name="sg_mix_out",
    )(h, proj, proj, v_gain, w_s, b_s, yb, w, g_next)


def _split3(x):
    x1 = x.astype(BF16)
    r1 = x - x1.astype(F32)
    x2 = r1.astype(BF16)
    x3 = (r1 - x2.astype(F32)).astype(BF16)
    return x1, x2, x3


def _group_row(x, group, idx):
    n, d = x.shape
    x3 = x.reshape(n // group, group, d)
    return jnp.broadcast_to(x3[:, idx:idx + 1, :], x3.shape).reshape(n, d)


def _mixfront_kernel(h_ref, d_ref, g_ref, w_ref, wg_ref, bg_ref, og_ref, wa_ref, wb_ref,
                     h1_ref, auv_ref, o_ref, wa16_ref, wb16_ref,
                     proj_ref, z_ref, proj_next_ref, z_next_ref, st_ref, *, tiles_per_seq):
    C = GLA_CHUNK
    step = pl.program_id(0)

    @pl.when(step == 0)
    def _():
        proj_ref[...] = jnp.zeros_like(proj_ref)
        z_ref[...] = jnp.zeros_like(z_ref)

    @pl.when((step == 0) | ((step - 1) % tiles_per_seq == 0))
    def _():
        st_ref[...] = jnp.zeros_like(st_ref)

    wa16_ref[...] = wa_ref[...].astype(BF16)
    wb16_ref[...] = wb_ref[...].astype(BF16)

    h1 = h_ref[...] + d_ref[...]
    h1_ref[...] = h1
    n = _rms_norm_rows(h1, g_ref[...]).astype(BF16)
    z_next_ref[...] = _dot(n, w_ref[:, MIX_MAIN:])
    n_slots = (MIX_TILE // C) * (GLA_HEADS + 1)
    n_slices = MIX_MAIN // MIX_SLICE
    slot_iter = iter(range(n_slots))

    def project_slice():
        slot = next(slot_iter)
        for i in range(-(-slot * n_slices // n_slots), -(-(slot + 1) * n_slices // n_slots)):
            lo = i * MIX_SLICE
            res = _dot(n, w_ref[:, lo:lo + MIX_SLICE])
            if lo < OFF_Q:
                auv_ref[:, lo:lo + MIX_SLICE] = res
            else:
                proj_next_ref[:, lo - OFF_Q:lo - OFF_Q + MIX_SLICE] = res

    pj = proj_ref
    zp = z_ref

    row = lax.broadcasted_iota(jnp.int32, (C, C), 0)
    col = lax.broadcasted_iota(jnp.int32, (C, C), 1)
    causal = row >= col
    tril = jnp.where(causal, 1.0, 0.0).astype(BF16)
    levels = []
    s = C // 2
    while s >= GLA_SUB:
        levels.append((s, ((row // s) % 2 == 1) & ((col // s) == (row // s) - 1)))
        s //= 2
    diag_off = col - (row // GLA_SUB) * GLA_SUB
    w_gate = wg_ref[...].astype(BF16)
    b_gate = bg_ref[...]
    g_scale = math.log2(math.e) / GLA_GATE_TAU

    for c in range(MIX_TILE // C):
        rows = slice(c * C, (c + 1) * C)
        project_slice()
        pre = _dot(zp[rows, :].astype(BF16), w_gate) + b_gate
        g = (jnp.minimum(pre, 0.0) - jnp.log1p(jnp.exp(-jnp.abs(pre)))) * g_scale
        g1, g2, g3 = _split3(g)
        b_all = _dot(tril, g1) + _dot(tril, g2) + _dot(tril, g3)

        for hd in range(GLA_HEADS):
            kc = slice(hd * GLA_DK, (hd + 1) * GLA_DK)
            vc = slice(hd * GLA_DV, (hd + 1) * GLA_DV)
            b = b_all[:, kc]
            project_slice()
            q = pj[rows, kc] * (GLA_DK ** -0.5)
            k = pj[rows, GLA_KW + hd * GLA_DK:GLA_KW + (hd + 1) * GLA_DK]
            v16 = pj[rows, 2 * GLA_KW + hd * GLA_DV:2 * GLA_KW + (hd + 1) * GLA_DV].astype(BF16)
            b_last = b[C - 1:C, :]

            st = st_ref[hd]
            o = _dot_nt((q * jnp.exp2(b)).astype(BF16), st.astype(BF16))

            sc = jnp.zeros((C, C), F32)
            for s, mask in levels:
                d = b - _group_row(b, 2 * s, s - 1)
                qs = q * jnp.exp2(jnp.minimum(d, 0.0))
                ks = k * jnp.exp2(jnp.minimum(-d, 0.0))
                sc = jnp.where(mask, _dot_nt(qs.astype(BF16), ks.astype(BF16)), sc)

            for j in range(GLA_SUB):
                e = jnp.exp2(b - _group_row(b, GLA_SUB, j))
                sj = jnp.sum(q * _group_row(k, GLA_SUB, j) * e, axis=-1, keepdims=True)
                sc = jnp.where(diag_off == j, sj, sc)
            sc = jnp.where(causal, sc, 0.0)
            o = o + _dot(sc.astype(BF16), v16)

            k_dec = k * jnp.exp2(b_last - b)
            st_ref[hd] = st * jnp.exp2(b_last) + _dot_tn(v16, k_dec.astype(BF16))

            ms = jnp.mean(o * o, axis=-1, keepdims=True)
            r = pj[rows, 2 * GLA_KW + GLA_VW + hd * GLA_DV:2 * GLA_KW + GLA_VW + (hd + 1) * GLA_DV]
            y = o * lax.rsqrt(ms + EPS) * og_ref[:, vc] * (r * jax.nn.sigmoid(r))
            o_ref[rows, vc] = y.astype(o_ref.dtype)

    proj_ref[...] = proj_next_ref[...]
    z_ref[...] = z_next_ref[...]


def _mixfront(h, d, g, w, w_gate, b_gate, o_gain, w_a, w_b, layer, t_len):
    n_rows = h.shape[0]
    tile = MIX_TILE
    n_tiles = n_rows // tile
    gla_cols = MIX_MAIN - OFF_Q
    w_rows = D_MODEL // n_tiles
    row_tile = lambda s: (jnp.minimum(s, n_tiles - 1), 0)
    return pl.pallas_call(
        functools.partial(_mixfront_kernel, tiles_per_seq=t_len // tile),
        grid=(n_tiles + 1,),
        in_specs=[
            pl.BlockSpec((tile, D_MODEL), lambda s: (jnp.minimum(s, n_tiles - 1), 0)),
            pl.BlockSpec((tile, D_MODEL), lambda s: (jnp.minimum(s, n_tiles - 1), 0)),
            pl.BlockSpec((None, 1, D_MODEL), lambda s: (layer, 0, 0)),
            pl.BlockSpec((None, D_MODEL, MIX_MAIN + GLA_GATE_RANK), lambda s: (layer, 0, 0),
                         pipeline_mode=pl.Buffered(1)),
            pl.BlockSpec((None, GLA_GATE_RANK, GLA_KW), lambda s: (layer, 0, 0)),
            pl.BlockSpec((None, 1, GLA_KW), lambda s: (layer, 0, 0)),
            pl.BlockSpec((None, 1, GLA_VW), lambda s: (layer, 0, 0)),
            pl.BlockSpec((None, w_rows, D_MODEL), lambda s: (layer,) + row_tile(s)),
            pl.BlockSpec((None, w_rows, D_MODEL), lambda s: (layer,) + row_tile(s)),
        ],
        out_specs=[
            pl.BlockSpec((tile, D_MODEL), lambda s: (jnp.minimum(s, n_tiles - 1), 0)),
            pl.BlockSpec((tile, OFF_Q), lambda s: (jnp.minimum(s, n_tiles - 1), 0)),
            pl.BlockSpec((tile, GLA_VW), lambda s: (jnp.maximum(s - 1, 0), 0)),
            pl.BlockSpec((w_rows, D_MODEL), row_tile),
            pl.BlockSpec((w_rows, D_MODEL), row_tile),
        ],
        out_shape=[
            jax.ShapeDtypeStruct((n_rows, D_MODEL), F32),
            jax.ShapeDtypeStruct((n_rows, OFF_Q), F32),
            jax.ShapeDtypeStruct((n_rows, GLA_VW), BF16),
            jax.ShapeDtypeStruct((D_MODEL, D_MODEL), BF16),
            jax.ShapeDtypeStruct((D_MODEL, D_MODEL), BF16),
        ],
        scratch_shapes=[
            pltpu.VMEM((tile, gla_cols), F32),
            pltpu.VMEM((tile, GLA_GATE_RANK), F32),
            pltpu.VMEM((tile, gla_cols), F32),
            pltpu.VMEM((tile, GLA_GATE_RANK), F32),
            pltpu.VMEM((GLA_HEADS, GLA_DV, GLA_DK), F32),
        ],
        compiler_params=pltpu.CompilerParams(
            dimension_semantics=("arbitrary",),
            vmem_limit_bytes=VMEM_LIMIT),
        name="mix_front",
    )(h, d, g, w, w_gate, b_gate, o_gain, w_a, w_b)


def _ple_kernel(h_hbm, d_ref, g_ref, wg_ref, p_ref, wp_ref, gn_ref, *refs, final_norm):
    if final_norm:
        o_ref, hring_ref, hsem_ref = refs
    else:
        o_ref, n_ref, hring_ref, hsem_ref = refs

    h = hring_ref[_ring_rows(h_hbm, hring_ref, hsem_ref)] + d_ref[...]
    n = _rms_norm_rows(h, g_ref[...]).astype(BF16)
    gate = jax.nn.sigmoid(_dot(n, wg_ref[...]))
    emb = _dot(p_ref[...].astype(BF16), wp_ref[...])
    out = h + gate * emb
    if final_norm:
        o_ref[...] = _rms_norm_rows(out, gn_ref[...])
    else:
        o_ref[...] = out
        n_ref[...] = _rms_norm_rows(out, gn_ref[...]).astype(n_ref.dtype)


def _ple(h, d, g, w_gate, p, w_proj, g_next, layer, final_norm):
    n_rows = h.shape[0]
    tm = PLE_TM
    const = pl.Buffered(1)
    row_spec = pl.BlockSpec((tm, D_MODEL), lambda i: (i, 0))
    if final_norm:
        out_specs = row_spec
        out_shape = jax.ShapeDtypeStruct((n_rows, D_MODEL), F32)
    else:
        out_specs = [row_spec, row_spec]
        out_shape = [jax.ShapeDtypeStruct((n_rows, D_MODEL), F32),
                     jax.ShapeDtypeStruct((n_rows, D_MODEL), BF16)]
    return pl.pallas_call(
        functools.partial(_ple_kernel, final_norm=final_norm),
        grid=(n_rows // tm,),
        in_specs=[
            pl.BlockSpec(memory_space=pl.ANY),
            row_spec,
            pl.BlockSpec((None, 1, D_MODEL), lambda i: (layer, 0, 0)),
            pl.BlockSpec((D_MODEL, D_MODEL), lambda i: (0, 0), pipeline_mode=const),
            pl.BlockSpec((None, tm, D_PLE), lambda i: (layer, i, 0)),
            pl.BlockSpec((None, D_PLE, D_MODEL), lambda i: (layer, 0, 0), pipeline_mode=const),
            pl.BlockSpec((1, D_MODEL), lambda i: (0, 0)),
        ],
        out_specs=out_specs,
        out_shape=out_shape,
        scratch_shapes=[pltpu.VMEM((ROW_RING, tm, D_MODEL), F32),
                        pltpu.SemaphoreType.DMA((ROW_RING,))],
        compiler_params=pltpu.CompilerParams(
            dimension_semantics=("arbitrary",),
            vmem_limit_bytes=VMEM_LIMIT),
        name="ple",
    )(h, d, g, w_gate, p, w_proj, g_next)


def kernel(x, p, ffn1_norm, w_ffn1_in, w_ffn1_out, mix_norm, w_mix_in, sg_v_gain, sg_w, sg_b,
           gla_w_gate, gla_b_gate, gla_o_gain, w_mix_out, ffn2_norm, w_ffn2_in, w_ffn2_out,
           ple_norm, w_ple_gate, w_ple_proj, final_norm):
    bsz, t_len, d_model = x.shape
    depth = p.shape[0]
    n_rows = bsz * t_len
    h = x.reshape(n_rows, d_model)

    def row(v):
        return v.reshape(depth, 1, v.shape[-1])

    ffn1_g, mix_g, ffn2_g, ple_g = row(ffn1_norm), row(mix_norm), row(ffn2_norm), row(ple_norm)
    w_mi = w_mix_in.astype(BF16)
    w_pp = w_ple_proj.astype(BF16)
    p2 = p.reshape(depth, n_rows, D_PLE)
    sg_gain, sg_bias = row(sg_v_gain), sg_b.reshape(depth, SG_HEADS, SG_CHUNK, 1)
    gla_bg, gla_og = row(gla_b_gate), row(gla_o_gain)
    final_g = final_norm.reshape(1, d_model)

    n = _norm(h, ffn1_g, 0)
    for i in range(depth):
        last = i == depth - 1
        d = _ffn(n, w_ffn1_in, w_ffn1_out, i)
        h, a_uv, y_b, w_mo, w_pg = _mixfront(h, d, mix_g, w_mi, gla_w_gate, gla_bg, gla_og,
                                             w_mix_out, w_ple_gate, i, t_len)
        h, n = _sg_mixout(h, a_uv, sg_gain, sg_w, sg_bias, y_b, w_mo, ffn2_g, i)
        d = _ffn(n, w_ffn2_in, w_ffn2_out, i)
        if last:
            h = _ple(h, d, ple_g, w_pg, p2, w_pp, final_g, i, final_norm=True)
        else:
            h, n = _ple(h, d, ple_g, w_pg, p2, w_pp, ffn1_norm[i + 1].reshape(1, d_model), i,
                        final_norm=False)
    return h.reshape(bsz, t_len, d_model)
```
